```python
import math
import jax, jax.numpy as jnp
from jax import lax
import numpy as np

D_MODEL = 1024
BATCH = 16
SEQ = 2048
DEPTH = 1
DEC_BATCH = 128
DEC_SEQ = 1
PAST_LEN = 8192
PAGE_SIZE = 128

MLA_HEADS = 8
MLA_NOPE_DIM = 64
MLA_ROPE_DIM = 32
MLA_V_DIM = 64
MLA_Q_RANK = 3 * D_MODEL // 8
MLA_KV_RANK = D_MODEL // 8
MLA_SCALE = (MLA_NOPE_DIM + MLA_ROPE_DIM) ** -0.5
ROPE_THETA = 10000.0
DIFF_HEADS = 4
DIFF_QK_DIM = 64
DIFF_V_DIM = 2 * DIFF_QK_DIM
DIFF_SCALE = DIFF_QK_DIM ** -0.5
SUBLN_EPS = 1e-5
REL_BUCKETS = 32
REL_MAX_DIST = 128
FFN_DIM = -(-8 * D_MODEL // (3 * 256)) * 256
NORM_EPS = 1e-6
Q_BLOCK = 128
IN_SIZES = (MLA_Q_RANK, MLA_KV_RANK, MLA_ROPE_DIM,
            DIFF_HEADS * 2 * DIFF_QK_DIM, DIFF_HEADS * 2 * DIFF_QK_DIM, DIFF_HEADS * DIFF_V_DIM,
            D_MODEL, D_MODEL)
IN_SPLITS = tuple(int(s) for s in np.cumsum(IN_SIZES)[:-1])
IN_DIM = int(sum(IN_SIZES))

kernel_name = 'hybrid_mla_diffattn_gated_decoder_step'


def rmsnorm(x, g, eps=NORM_EPS):
    xf = x.astype(jnp.float32)
    y = xf * lax.rsqrt(jnp.mean(xf * xf, axis=-1, keepdims=True) + eps)
    return (y * g.astype(jnp.float32)).astype(x.dtype)


def rope(x, pos):
    half = x.shape[-1] // 2
    inv = ROPE_THETA ** (-jnp.arange(half, dtype=jnp.float32) / half)
    ang = pos.astype(jnp.float32)[:, None] * inv
    ang = ang.reshape((1, x.shape[1]) + (1,) * (x.ndim - 3) + (half,))
    cos, sin = jnp.cos(ang), jnp.sin(ang)
    xf = x.astype(jnp.float32)
    x1, x2 = xf[..., :half], xf[..., half:]
    return jnp.concatenate([x1 * cos - x2 * sin, x1 * sin + x2 * cos], axis=-1).astype(x.dtype)


def rel_bucket(qpos, kpos):
    n = jnp.maximum(qpos[:, None] - kpos[None, :], 0)
    max_exact = REL_BUCKETS // 2
    nf = jnp.maximum(n, 1).astype(jnp.float32)
    large = max_exact + (jnp.log(nf / max_exact) / math.log(REL_MAX_DIST / max_exact)
                         * (REL_BUCKETS - max_exact)).astype(jnp.int32)
    large = jnp.minimum(large, REL_BUCKETS - 1)
    return jnp.where(n < max_exact, n, large)


def rel_bias(table, qpos, kpos):
    return table[rel_bucket(qpos, kpos)].astype(jnp.float32).transpose(2, 0, 1)


def softmax_stats(s, pv):
    m = jnp.max(s, axis=-1)
    p = jnp.exp(s - m[..., None])
    return (m, jnp.sum(p, axis=-1), pv(p))


def softmax_merge(stats, s, pv):
    m, l, acc = stats
    m_new = jnp.maximum(m, jnp.max(s, axis=-1))
    a = jnp.exp(m - m_new)
    p = jnp.exp(s - m_new[..., None])
    return (m_new, l * a + jnp.sum(p, axis=-1), acc * a[..., None] + pv(p))


def mixer_inputs(h, pos, w_in, mla_q_norm_g, mla_w_uq, mla_kv_norm_g, mla_w_uk):
    B, T, _ = h.shape
    z = h @ w_in
    q_lat, kv_lat, kpe_raw, dq, dk, dv, ga, gb = jnp.split(z, IN_SPLITS, axis=-1)
    q = (rmsnorm(q_lat, mla_q_norm_g) @ mla_w_uq).reshape(B, T, MLA_HEADS, MLA_NOPE_DIM + MLA_ROPE_DIM)
    q_nope = q[..., :MLA_NOPE_DIM]
    q_pe = rope(q[..., MLA_NOPE_DIM:], pos)
    q_abs = jnp.einsum('bthn,rhn->bthr', q_nope, mla_w_uk)
    mla_q = jnp.concatenate([q_abs, q_pe], axis=-1) * MLA_SCALE
    c_kv = rmsnorm(kv_lat, mla_kv_norm_g)
    k_pe = rope(kpe_raw, pos)
    diff_q = dq.reshape(B, T, DIFF_HEADS, 2, DIFF_QK_DIM) * DIFF_SCALE
    diff_k = dk.reshape(B, T, DIFF_HEADS, 2, DIFF_QK_DIM)
    diff_v = dv.reshape(B, T, DIFF_HEADS, DIFF_V_DIM)
    return mla_q, c_kv, k_pe, diff_q, diff_k, diff_v, jax.nn.sigmoid(ga), jax.nn.sigmoid(gb)


def mla_attend_prompt(mla_q, c_kv, k_pe):
    B, T = mla_q.shape[:2]
    kpos = jnp.arange(T)

    def one_block(i):
        q = lax.dynamic_slice_in_dim(mla_q, i * Q_BLOCK, Q_BLOCK, axis=1)
        s = (jnp.einsum('bqhr,bkr->bhqk', q[..., :MLA_KV_RANK], c_kv)
             + jnp.einsum('bqhp,bkp->bhqk', q[..., MLA_KV_RANK:], k_pe)).astype(jnp.float32)
        qpos = i * Q_BLOCK + jnp.arange(Q_BLOCK)
        s = jnp.where(kpos[None, :] <= qpos[:, None], s, -jnp.inf)
        p = jax.nn.softmax(s, axis=-1).astype(c_kv.dtype)
        return jnp.einsum('bhqk,bkr->bqhr', p, c_kv)

    o = lax.map(one_block, jnp.arange(T // Q_BLOCK))
    return o.transpose(1, 0, 2, 3, 4).reshape(B, T, MLA_HEADS, MLA_KV_RANK)


def diff_attend_prompt(q, k, v, table):
    B, T = q.shape[:2]
    kpos = jnp.arange(T)

    def one_block(i):
        qb = lax.dynamic_slice_in_dim(q, i * Q_BLOCK, Q_BLOCK, axis=1)
        qpos = i * Q_BLOCK + jnp.arange(Q_BLOCK)
        s = jnp.einsum('bqhjd,bkhjd->bhjqk', qb, k).astype(jnp.float32) \
            + rel_bias(table, qpos, kpos)[None, :, None]
        s = jnp.where(kpos[None, :] <= qpos[:, None], s, -jnp.inf)
        p = jax.nn.softmax(s, axis=-1).astype(v.dtype)
        return jnp.einsum('bhjqk,bkhd->bqhjd', p, v)

    o = lax.map(one_block, jnp.arange(T // Q_BLOCK))
    return o.transpose(1, 0, 2, 3, 4, 5).reshape(B, T, DIFF_HEADS, 2, DIFF_V_DIM)


def mla_attend_sample(mla_q, c_kv, k_pe, cache_ckv, cache_kpe, page_table, layer):
    T = mla_q.shape[1]
    qa, qp = mla_q[..., :MLA_KV_RANK], mla_q[..., MLA_KV_RANK:]

    def scores(ckv, kpe):
        return (jnp.einsum('bqhr,bkr->bhqk', qa, ckv)
                + jnp.einsum('bqhp,bkp->bhqk', qp, kpe)).astype(jnp.float32)

    def pv(ckv):
        return lambda p: jnp.einsum('bhqk,bkr->bhqr', p.astype(ckv.dtype), ckv).astype(jnp.float32)

    causal = jnp.tril(jnp.ones((T, T), dtype=bool))
    stats = softmax_stats(jnp.where(causal, scores(c_kv, k_pe), -jnp.inf), pv(c_kv))

    def step(stats, i):
        pages = page_table[:, i]
        ckv = cache_ckv[layer, pages]
        kpe = cache_kpe[layer, pages]
        return softmax_merge(stats, scores(ckv, kpe), pv(ckv)), None

    (m, l, acc), _ = lax.scan(step, stats, jnp.arange(page_table.shape[1]))
    o = acc / l[..., None]
    return o.transpose(0, 2, 1, 3).astype(c_kv.dtype)


def diff_attend_sample(q, k, v, pos, cache_k, cache_v, page_table, table, layer):
    T = q.shape[1]

    def scores(kb, kpos):
        return jnp.einsum('bqhjd,bkhjd->bhjqk', q, kb).astype(jnp.float32) \
            + rel_bias(table, pos, kpos)[None, :, None]

    def pv(vb):
        return lambda p: jnp.einsum('bhjqk,bkhd->bhjqd', p.astype(vb.dtype), vb).astype(jnp.float32)

    causal = jnp.tril(jnp.ones((T, T), dtype=bool))
    stats = softmax_stats(jnp.where(causal, scores(k, pos), -jnp.inf), pv(v))

    def step(stats, i):
        pages = page_table[:, i]
        kb = cache_k[layer, pages]
        vb = cache_v[layer, pages]
        kpos = i * PAGE_SIZE + jnp.arange(PAGE_SIZE)
        return softmax_merge(stats, scores(kb, kpos), pv(vb)), None

    (m, l, acc), _ = lax.scan(step, stats, jnp.arange(page_table.shape[1]))
    o = acc / l[..., None]
    return o.transpose(0, 3, 1, 2, 4).astype(v.dtype)


def decoder_layer(x, pos, attend_mla, attend_diff, lam_init, weights):
    (norm_attn_g, w_in, mla_q_norm_g, mla_w_uq, mla_kv_norm_g, mla_w_uk, mla_w_uv,
     lq1, lk1, lq2, lk2, diff_subln_g, w_branch_mla, w_branch_diff, w_o,
     norm_ffn_g, w_ffn_gate, w_ffn_up, w_ffn_down) = weights
    B, T, _ = x.shape
    h = rmsnorm(x, norm_attn_g)
    mla_q, c_kv, k_pe, dq, dk, dv, gate_mla, gate_diff = mixer_inputs(
        h, pos, w_in, mla_q_norm_g, mla_w_uq, mla_kv_norm_g, mla_w_uk)
    o_lat = attend_mla(mla_q, c_kv, k_pe)
    o_mla = jnp.einsum('bthr,rhv->bthv', o_lat, mla_w_uv).reshape(B, T, MLA_HEADS * MLA_V_DIM)
    lam = (jnp.exp(jnp.sum(lq1.astype(jnp.float32) * lk1.astype(jnp.float32)))
           - jnp.exp(jnp.sum(lq2.astype(jnp.float32) * lk2.astype(jnp.float32))) + lam_init)
    o_pair = attend_diff(dq, dk, dv)
    o_d = o_pair[..., 0, :].astype(jnp.float32) - lam * o_pair[..., 1, :].astype(jnp.float32)
    o_diff = (rmsnorm(o_d, diff_subln_g, SUBLN_EPS) * (1.0 - lam_init)).astype(x.dtype)
    o_diff = o_diff.reshape(B, T, DIFF_HEADS * DIFF_V_DIM)
    merged = gate_mla * (o_mla @ w_branch_mla) + gate_diff * (o_diff @ w_branch_diff)
    x = x + merged @ w_o
    h2 = rmsnorm(x, norm_ffn_g)
    x = x + (jax.nn.silu(h2 @ w_ffn_gate) * (h2 @ w_ffn_up)) @ w_ffn_down
    return x, (c_kv, k_pe, dk, dv)


def setup_inputs(seed: int = 0) -> dict:
    key = jax.random.key(seed)
    ks = jax.random.split(key, 32)
    f32 = jnp.float32

    def nrm(k, shape, scale):
        return jax.random.normal(k, shape, f32) * scale

    def gain(k, shape):
        return 1.0 + 0.05 * jax.random.normal(k, shape, f32)

    n_pages = PAST_LEN // PAGE_SIZE
    n_used = DEC_BATCH * n_pages
    n_pool = n_used + max(1, n_used // 4)
    page_table = jax.random.permutation(ks[0], n_pool)[:n_used].reshape(DEC_BATCH, n_pages).astype(jnp.int32)
    qk_head = MLA_NOPE_DIM + MLA_ROPE_DIM
    return {
        'x_prompt': nrm(ks[1], (BATCH, SEQ, D_MODEL), 1.0),
        'x_sample': nrm(ks[2], (DEC_BATCH, DEC_SEQ, D_MODEL), 1.0),
        'cache_mla_ckv': nrm(ks[3], (DEPTH, n_pool, PAGE_SIZE, MLA_KV_RANK), 1.0),
        'cache_mla_kpe': nrm(ks[4], (DEPTH, n_pool, PAGE_SIZE, MLA_ROPE_DIM), 1.0),
        'cache_diff_k': nrm(ks[5], (DEPTH, n_pool, PAGE_SIZE, DIFF_HEADS, 2, DIFF_QK_DIM), 1.0),
        'cache_diff_v': nrm(ks[6], (DEPTH, n_pool, PAGE_SIZE, DIFF_HEADS, DIFF_V_DIM), 1.0),
        'page_table': page_table,
        'rel_bias_table': nrm(ks[7], (REL_BUCKETS, DIFF_HEADS), 0.3),
        'norm_attn_g': gain(ks[8], (DEPTH, D_MODEL)),
        'w_in': nrm(ks[9], (DEPTH, D_MODEL, IN_DIM), D_MODEL ** -0.5),
        'mla_q_norm_g': gain(ks[10], (DEPTH, MLA_Q_RANK)),
        'mla_w_uq': nrm(ks[11], (DEPTH, MLA_Q_RANK, MLA_HEADS * qk_head), MLA_Q_RANK ** -0.5),
        'mla_kv_norm_g': gain(ks[12], (DEPTH, MLA_KV_RANK)),
        'mla_w_uk': nrm(ks[13], (DEPTH, MLA_KV_RANK, MLA_HEADS, MLA_NOPE_DIM), MLA_KV_RANK ** -0.5),
        'mla_w_uv': nrm(ks[14], (DEPTH, MLA_KV_RANK, MLA_HEADS, MLA_V_DIM), MLA_KV_RANK ** -0.5),
        'diff_lambda_q1': nrm(ks[15], (DEPTH, DIFF_QK_DIM), 0.1),
        'diff_lambda_k1': nrm(ks[16], (DEPTH, DIFF_QK_DIM), 0.1),
        'diff_lambda_q2': nrm(ks[17], (DEPTH, DIFF_QK_DIM), 0.1),
        'diff_lambda_k2': nrm(ks[18], (DEPTH, DIFF_QK_DIM), 0.1),
        'diff_subln_g': gain(ks[19], (DEPTH, DIFF_V_DIM)),
        'w_branch_mla': nrm(ks[20], (DEPTH, MLA_HEADS * MLA_V_DIM, D_MODEL), (MLA_HEADS * MLA_V_DIM) ** -0.5),
        'w_branch_diff': nrm(ks[21], (DEPTH, DIFF_HEADS * DIFF_V_DIM, D_MODEL), (DIFF_HEADS * DIFF_V_DIM) ** -0.5),
        'w_o': nrm(ks[22], (DEPTH, D_MODEL, D_MODEL), D_MODEL ** -0.5),
        'norm_ffn_g': gain(ks[23], (DEPTH, D_MODEL)),
        'w_ffn_gate': nrm(ks[24], (DEPTH, D_MODEL, FFN_DIM), D_MODEL ** -0.5),
        'w_ffn_up': nrm(ks[25], (DEPTH, D_MODEL, FFN_DIM), D_MODEL ** -0.5),
        'w_ffn_down': nrm(ks[26], (DEPTH, FFN_DIM, D_MODEL), FFN_DIM ** -0.5),
        'norm_final_g': gain(ks[27], (D_MODEL,)),
    }


def reference(x_prompt, x_sample, cache_mla_ckv, cache_mla_kpe, cache_diff_k, cache_diff_v, page_table,
              rel_bias_table, norm_attn_g, w_in, mla_q_norm_g, mla_w_uq, mla_kv_norm_g, mla_w_uk, mla_w_uv,
              diff_lambda_q1, diff_lambda_k1, diff_lambda_q2, diff_lambda_k2, diff_subln_g,
              w_branch_mla, w_branch_diff, w_o, norm_ffn_g, w_ffn_gate, w_ffn_up, w_ffn_down, norm_final_g):
    pos_p = jnp.arange(x_prompt.shape[1], dtype=jnp.int32)
    past_len = page_table.shape[1] * PAGE_SIZE
    pos_s = past_len + jnp.arange(x_sample.shape[1], dtype=jnp.int32)
    hp, hs = x_prompt, x_sample
    p_ckv, p_kpe, p_dk, p_dv = [], [], [], []
    s_ckv, s_kpe, s_dk, s_dv = [], [], [], []
    for layer in range(DEPTH):
        lam_init = 0.8 - 0.6 * math.exp(-0.3 * layer)
        weights = (norm_attn_g[layer], w_in[layer], mla_q_norm_g[layer], mla_w_uq[layer],
                   mla_kv_norm_g[layer], mla_w_uk[layer], mla_w_uv[layer],
                   diff_lambda_q1[layer], diff_lambda_k1[layer], diff_lambda_q2[layer], diff_lambda_k2[layer],
                   diff_subln_g[layer], w_branch_mla[layer], w_branch_diff[layer], w_o[layer],
                   norm_ffn_g[layer], w_ffn_gate[layer], w_ffn_up[layer], w_ffn_down[layer])
        hp, st_p = decoder_layer(
            hp, pos_p, mla_attend_prompt,
            lambda q, k, v: diff_attend_prompt(q, k, v, rel_bias_table),
            lam_init, weights)
        hs, st_s = decoder_layer(
            hs, pos_s,
            lambda q, c, kp: mla_attend_sample(q, c, kp, cache_mla_ckv, cache_mla_kpe, page_table, layer),
            lambda q, k, v: diff_attend_sample(q, k, v, pos_s, cache_diff_k, cache_diff_v, page_table,
                                               rel_bias_table, layer),
            lam_init, weights)
        p_ckv.append(st_p[0]); p_kpe.append(st_p[1]); p_dk.append(st_p[2]); p_dv.append(st_p[3])
        s_ckv.append(st_s[0]); s_kpe.append(st_s[1]); s_dk.append(st_s[2]); s_dv.append(st_s[3])
    y_prompt = rmsnorm(hp, norm_final_g)
    y_sample = rmsnorm(hs, norm_final_g)
    return (y_prompt, y_sample,
            jnp.stack(p_ckv), jnp.stack(p_kpe), jnp.stack(p_dk), jnp.stack(p_dv),
            jnp.stack(s_ckv), jnp.stack(s_kpe), jnp.stack(s_dk), jnp.stack(s_dv))
```

```python
import functools
import math

import numpy as np
import jax
import jax.numpy as jnp
from jax import lax
from jax.experimental import pallas as pl
from jax.experimental.pallas import tpu as pltpu

F32 = jnp.float32
BF16 = jnp.bfloat16

D_MODEL = 1024
PAGE = 128
MLA_HEADS = 8
MLA_NOPE = 64
MLA_ROPE = 32
MLA_V = 64
MLA_QR = 384
MLA_KVR = 128
MLA_SCALE = (MLA_NOPE + MLA_ROPE) ** -0.5
ROPE_THETA = 10000.0
DIFF_HEADS = 4
DIFF_QK = 64
DIFF_V = 128
DIFF_SCALE = DIFF_QK ** -0.5
SUBLN_EPS = 1e-5
REL_BUCKETS = 32
REL_MAX_DIST = 128
FFN_DIM = 2816
NORM_EPS = 1e-6
LAM_INIT = 0.8 - 0.6 * math.exp(-0.3 * 0)

LANES = 128
QCAT = 2 * LANES
NEG = -1e30
VMEM_LIMIT = 56 * 1024 * 1024

C_QLAT = 0
C_KV = 384
C_KPE = 512
C_KPESW = 640
C_DQ = 768
C_DK = 1280
C_DV = 1792
C_GA = 2304
C_GB = 3328
C_END = 4352

BQ = 256
BK = 256
DEC_PAGES = 8


def _cparams(sem):
    return pltpu.CompilerParams(dimension_semantics=sem, vmem_limit_bytes=VMEM_LIMIT)


def _const_spec(shape):
    nd = len(shape)
    return pl.BlockSpec(shape, lambda *_: (0,) * nd, pipeline_mode=pl.Buffered(1))


def _rope_tables_kernel(pos_ref, inv_ref, cq_ref, sq_ref, ck_ref, sk_ref):
    ang = pos_ref[...] * inv_ref[...]
    c = jnp.cos(ang)
    s = jnp.sin(ang)
    lane = lax.broadcasted_iota(jnp.int32, ang.shape, 1)
    first_half = (lane & (MLA_ROPE - 1)) < (MLA_ROPE // 2)
    s_signed = jnp.where(first_half, -s, s)
    cq_ref[...] = c
    sq_ref[...] = s_signed
    kmask = lax.broadcasted_iota(jnp.int32, ck_ref.shape, 1) < MLA_ROPE
    ck_ref[...] = jnp.where(kmask, c[:, :LANES], 0.0)
    sk_ref[...] = jnp.where(kmask, s_signed[:, :LANES], 0.0)


def _rope_tables(pos):
    n = pos.shape[0]
    half = MLA_ROPE // 2
    inv = (ROPE_THETA ** (-jnp.arange(half, dtype=F32) / half))
    inv = jnp.tile(inv, 2 * MLA_HEADS).reshape(1, 2 * LANES)
    return pl.pallas_call(
        _rope_tables_kernel,
        out_shape=(jax.ShapeDtypeStruct((n, 2 * LANES), F32), jax.ShapeDtypeStruct((n, 2 * LANES), F32),
                   jax.ShapeDtypeStruct((n, LANES), F32), jax.ShapeDtypeStruct((n, LANES), F32)),
        name="rope_tables",
    )(pos.astype(F32).reshape(n, 1), inv)


def _bucket(n):
    max_exact = REL_BUCKETS // 2
    nf = jnp.maximum(n, 1).astype(F32)
    large = max_exact + (jnp.log(nf / max_exact) / math.log(REL_MAX_DIST / max_exact)
                         * (REL_BUCKETS - max_exact)).astype(jnp.int32)
    large = jnp.minimum(large, REL_BUCKETS - 1)
    return jnp.where(n < max_exact, n, large)


def _bias_kernel(tab_ref, bp_ref, bd_ref, bs_ref, *, past_len):
    kidx = lax.broadcasted_iota(jnp.int32, (BK, BQ), 0)
    qidx = lax.broadcasted_iota(jnp.int32, (BK, BQ), 1)
    for d in range(3):
        rel = d * BQ + qidx - kidx
        b = _bucket(jnp.maximum(rel, 0))
        for h in range(DIFF_HEADS):
            val = jnp.zeros((BK, BQ), F32)
            for t in range(REL_BUCKETS):
                val = jnp.where(b == t, tab_ref[t, h], val)
            if d == 0:
                val = jnp.where(rel >= 0, val, NEG)
            bp_ref[d * DIFF_HEADS + h] = val
    bp_ref[3 * DIFF_HEADS] = jnp.where(qidx - kidx >= 0, 0.0, NEG)
    kpos = lax.broadcasted_iota(jnp.int32, (2 * DIFF_HEADS, past_len), 1)
    row = lax.broadcasted_iota(jnp.int32, (2 * DIFF_HEADS, past_len), 0)
    b = _bucket(jnp.maximum(past_len - kpos, 0))
    val = jnp.zeros((2 * DIFF_HEADS, past_len), F32)
    for h in range(DIFF_HEADS):
        for t in range(REL_BUCKETS):
            val = jnp.where((b == t) & ((row >> 1) == h), tab_ref[t, h], val)
    bd_ref[...] = val
    rows = lax.broadcasted_iota(jnp.int32, (2 * DIFF_HEADS, LANES), 0)
    b0 = _bucket(jnp.zeros((2 * DIFF_HEADS, LANES), jnp.int32))
    vs = jnp.zeros((2 * DIFF_HEADS, LANES), F32)
    for h in range(DIFF_HEADS):
        for t in range(REL_BUCKETS):
            vs = jnp.where((b0 == t) & ((rows >> 1) == h), tab_ref[t, h], vs)
    bs_ref[...] = vs


def _bias_tables(table, past_len):
    return pl.pallas_call(
        functools.partial(_bias_kernel, past_len=past_len),
        in_specs=[pl.BlockSpec(memory_space=pltpu.SMEM)],
        out_shape=(jax.ShapeDtypeStruct((3 * DIFF_HEADS + 1, BK, BQ), F32),
                   jax.ShapeDtypeStruct((2 * DIFF_HEADS, past_len), F32),
                   jax.ShapeDtypeStruct((2 * DIFF_HEADS, LANES), F32)),
        compiler_params=pltpu.CompilerParams(vmem_limit_bytes=VMEM_LIMIT),
        name="bias_tables",
    )(table.astype(F32))


def _fold_kernel(nope_ref, uk_ref, uv_ref, bm_ref, wfold_ref, wf2_ref):
    for h in range(MLA_HEADS):
        wfold_ref[:, h * LANES:(h + 1) * LANES] = lax.dot_general(
            nope_ref[h], uk_ref[h], (((1,), (1,)), ((), ())),
            preferred_element_type=F32, precision=lax.Precision.HIGHEST)
        wf2_ref[h * LANES:(h + 1) * LANES, :] = jnp.dot(
            uv_ref[h], bm_ref[h], preferred_element_type=F32, precision=lax.Precision.HIGHEST)


def _fold_weights(w_uq, w_uk, w_uv, w_bm):
    uq = w_uq.reshape(MLA_QR, MLA_HEADS, MLA_NOPE + MLA_ROPE)
    nope = jnp.transpose(uq[:, :, :MLA_NOPE], (1, 0, 2))
    uk = jnp.transpose(w_uk, (1, 0, 2))
    uv = jnp.transpose(w_uv, (1, 0, 2))
    bm = w_bm.reshape(MLA_HEADS, MLA_V, D_MODEL)
    return pl.pallas_call(
        _fold_kernel,
        out_shape=(jax.ShapeDtypeStruct((MLA_QR, MLA_HEADS * LANES), F32),
                   jax.ShapeDtypeStruct((MLA_HEADS * LANES, D_MODEL), F32)),
        compiler_params=pltpu.CompilerParams(vmem_limit_bytes=VMEM_LIMIT),
        name="fold_weights",
    )(nope, uk, uv, bm)


def _rms(x, g, eps):
    return x * lax.rsqrt(jnp.mean(x * x, axis=-1, keepdims=True) + eps) * g


def _proj_kernel(x_ref, g_ref, win_ref, gq_ref, wq2_ref, gkv_ref, cq_ref, sq_ref, ck_ref, sk_ref,
                 ckv_ref, kpe_ref, dk_ref, dv_ref, kvcat_ref, ckvt_ref, qcat_ref, dq_ref, dk16_ref, dvt_ref,
                 ga_ref, gb_ref):
    tm = x_ref.shape[0]
    h = _rms(x_ref[...], g_ref[...], NORM_EPS).astype(BF16)

    def seg(lo, hi):
        return jnp.dot(h, win_ref[:, lo:hi], preferred_element_type=F32)

    qn = _rms(seg(C_QLAT, C_KV), gq_ref[...], NORM_EPS).astype(BF16)
    zq = jnp.dot(qn, wq2_ref[...], preferred_element_type=F32)
    nq = MLA_HEADS * LANES
    qpe = (zq[:, nq:nq + 2 * LANES] * cq_ref[...] + zq[:, nq + 2 * LANES:] * sq_ref[...]) * MLA_SCALE
    lane = lax.broadcasted_iota(jnp.int32, (tm, LANES), 1)
    per_block = LANES // MLA_ROPE
    for hd in range(MLA_HEADS):
        blk = qpe[:, (hd // per_block) * LANES:(hd // per_block + 1) * LANES]
        sh = (hd % per_block) * MLA_ROPE
        if sh:
            blk = pltpu.roll(blk, LANES - sh, axis=1)
        blk = jnp.where(lane < MLA_ROPE, blk, 0.0)
        qcat_ref[:, hd * QCAT:hd * QCAT + LANES] = (zq[:, hd * LANES:(hd + 1) * LANES] * MLA_SCALE).astype(BF16)
        qcat_ref[:, hd * QCAT + LANES:(hd + 1) * QCAT] = blk.astype(BF16)

    ckv = _rms(seg(C_KV, C_KPE), gkv_ref[...], NORM_EPS)
    kpe = seg(C_KPE, C_KPESW) * ck_ref[...] + seg(C_KPESW, C_DQ) * sk_ref[...]
    ckv_ref[...] = ckv
    kpe_ref[...] = kpe[:, :MLA_ROPE]
    kvcat_ref[:, :LANES] = ckv.astype(BF16)
    kvcat_ref[:, LANES:] = kpe.astype(BF16)
    ckvt_ref[0] = ckv.T.astype(BF16)

    dq_ref[...] = seg(C_DQ, C_DK).astype(BF16)
    dk = seg(C_DK, C_DV)
    dk_ref[...] = dk
    dk16_ref[...] = dk.astype(BF16)
    dv = seg(C_DV, C_GA)
    dv_ref[...] = dv
    for hd in range(DIFF_HEADS):
        dvt_ref[0, hd] = dv[:, hd * DIFF_V:(hd + 1) * DIFF_V].T.astype(BF16)

    ga_ref[...] = jax.nn.sigmoid(seg(C_GA, C_GB))
    gb_ref[...] = jax.nn.sigmoid(seg(C_GB, C_END))


def _proj(x, tables, tm, weights):
    g_attn, w_in_r, g_q, w_q2, g_kv = weights
    cq, sq, ck, sk = tables
    n = x.shape[0]
    nb = n // tm
    tb = cq.shape[0] // tm
    row = lambda i: (i, 0)
    trow = lambda i: (i % tb, 0)
    in_specs = [
        pl.BlockSpec((tm, D_MODEL), row),
        _const_spec((1, D_MODEL)),
        _const_spec((D_MODEL, C_END)),
        _const_spec((1, MLA_QR)),
        _const_spec((MLA_QR, w_q2.shape[1])),
        _const_spec((1, MLA_KVR)),
        pl.BlockSpec((tm, 2 * LANES), trow), pl.BlockSpec((tm, 2 * LANES), trow),
        pl.BlockSpec((tm, LANES), trow), pl.BlockSpec((tm, LANES), trow),
    ]
    out_shape = (
        jax.ShapeDtypeStruct((n, MLA_KVR), F32),
        jax.ShapeDtypeStruct((n, MLA_ROPE), F32),
        jax.ShapeDtypeStruct((n, 2 * DIFF_HEADS * DIFF_QK), F32),
        jax.ShapeDtypeStruct((n, DIFF_HEADS * DIFF_V), F32),
        jax.ShapeDtypeStruct((n, QCAT), BF16),
        jax.ShapeDtypeStruct((nb, MLA_KVR, tm), BF16),
        jax.ShapeDtypeStruct((n, MLA_HEADS * QCAT), BF16),
        jax.ShapeDtypeStruct((n, 2 * DIFF_HEADS * DIFF_QK), BF16),
        jax.ShapeDtypeStruct((n, 2 * DIFF_HEADS * DIFF_QK), BF16),
        jax.ShapeDtypeStruct((nb, DIFF_HEADS, DIFF_V, tm), BF16),
        jax.ShapeDtypeStruct((n, D_MODEL), F32),
        jax.ShapeDtypeStruct((n, D_MODEL), F32),
    )
    out_specs = (
        pl.BlockSpec((tm, MLA_KVR), row),
        pl.BlockSpec((tm, MLA_ROPE), row),
        pl.BlockSpec((tm, 512), row),
        pl.BlockSpec((tm, 512), row),
        pl.BlockSpec((tm, QCAT), row),
        pl.BlockSpec((1, MLA_KVR, tm), lambda i: (i, 0, 0)),
        pl.BlockSpec((tm, MLA_HEADS * QCAT), row),
        pl.BlockSpec((tm, 512), row),
        pl.BlockSpec((tm, 512), row),
        pl.BlockSpec((1, DIFF_HEADS, DIFF_V, tm), lambda i: (i, 0, 0, 0)),
        pl.BlockSpec((tm, D_MODEL), row),
        pl.BlockSpec((tm, D_MODEL), row),
    )
    return pl.pallas_call(
        _proj_kernel,
        grid=(nb,),
        in_specs=in_specs,
        out_specs=out_specs,
        out_shape=out_shape,
        compiler_params=_cparams(("parallel",)),
        name="proj_in",
    )(x, g_attn, w_in_r, g_q, w_q2, g_kv, cq, sq, ck, sk)


def _lambda(lq1_ref, lk1_ref, lq2_ref, lk2_ref):
    a = jnp.sum(lq1_ref[...] * lk1_ref[...], axis=-1, keepdims=True)
    b = jnp.sum(lq2_ref[...] * lk2_ref[...], axis=-1, keepdims=True)
    return jnp.exp(a) - jnp.exp(b) + LAM_INIT


N_STATE = MLA_HEADS + 2 * DIFF_HEADS


def _attn_kernel(qcat_ref, dq_ref, kv_ref, ckvt_ref, dk_ref, dvt_ref, bias_ref,
                 lq1_ref, lk1_ref, lq2_ref, lk2_ref, subg_ref,
                 olat_ref, odiff_ref, qd_scr, m_scr, l_scr, acc_scr):
    i = pl.program_id(1)

    lane = lax.broadcasted_iota(jnp.int32, (BQ, LANES), 1)
    for hd in range(DIFF_HEADS):
        dqh = dq_ref[:, hd * LANES:(hd + 1) * LANES]
        qd_scr[2 * hd] = jnp.where(lane < DIFF_QK, dqh, jnp.zeros_like(dqh))
        qd_scr[2 * hd + 1] = jnp.where(lane >= DIFF_QK, dqh, jnp.zeros_like(dqh))

    m_scr[...] = jnp.full(m_scr.shape, NEG, F32)
    l_scr[...] = jnp.zeros(l_scr.shape, F32)
    acc_scr[...] = jnp.zeros(acc_scr.shape, F32)

    def update(idx, k_op, q_op, vt_op, bias):
        st = lax.dot_general(k_op, q_op, (((1,), (1,)), ((), ())), preferred_element_type=F32)
        if bias is not None:
            st = st + bias
        m_old = m_scr[idx]
        m_new = jnp.maximum(m_old, jnp.max(st, axis=0, keepdims=True))
        a = jnp.exp(m_old - m_new)
        pt = jnp.exp(st - m_new)
        l_scr[idx] = l_scr[idx] * a + jnp.sum(pt, axis=0, keepdims=True)
        acc_scr[idx] = acc_scr[idx] * a + jnp.dot(vt_op, pt.astype(BF16), preferred_element_type=F32)
        m_scr[idx] = m_new

    def block(j, diag):
        koff = pl.multiple_of(j * BK, BK)
        kvb = kv_ref[pl.ds(koff, BK), :]
        vtb = ckvt_ref[j]
        for hd in range(MLA_HEADS):
            bias = bias_ref[3 * DIFF_HEADS] if diag else None
            update(hd, kvb, qcat_ref[:, hd * QCAT:(hd + 1) * QCAT], vtb, bias)
        d = 0 if diag else jnp.minimum(i - j, 2)
        for hd in range(DIFF_HEADS):
            kb = dk_ref[pl.ds(koff, BK), hd * LANES:(hd + 1) * LANES]
            vt = dvt_ref[j, hd]
            bias = bias_ref[d * DIFF_HEADS + hd]
            for half in range(2):
                update(MLA_HEADS + 2 * hd + half, kb, qd_scr[2 * hd + half], vt, bias)

    def body(j, carry):
        block(j, False)
        return carry

    lax.fori_loop(0, i, body, 0)
    block(i, True)

    for hd in range(MLA_HEADS):
        o = acc_scr[hd] / l_scr[hd]
        olat_ref[:, hd * LANES:(hd + 1) * LANES] = o.T.astype(BF16)
    lam = _lambda(lq1_ref, lk1_ref, lq2_ref, lk2_ref)
    for hd in range(DIFF_HEADS):
        i0 = MLA_HEADS + 2 * hd
        od = acc_scr[i0] / l_scr[i0] - lam * (acc_scr[i0 + 1] / l_scr[i0 + 1])
        y = od * lax.rsqrt(jnp.mean(od * od, axis=0, keepdims=True) + SUBLN_EPS) * subg_ref[...]
        odiff_ref[:, hd * LANES:(hd + 1) * LANES] = (y * (1.0 - LAM_INIT)).T.astype(BF16)


def _attn_prompt(qcat, dq, kvcat, ckvt, dk16, dvt, bias, lam_vecs, subg_col, batch, seq):
    nq = seq // BQ
    nk = seq // BK
    n = batch * seq
    qrow = lambda b, i: (b * nq + i, 0)
    in_specs = [
        pl.BlockSpec((BQ, MLA_HEADS * QCAT), qrow),
        pl.BlockSpec((BQ, 512), qrow),
        pl.BlockSpec((seq, QCAT), lambda b, i: (b, 0)),
        pl.BlockSpec((nk, MLA_KVR, BK), lambda b, i: (b, 0, 0)),
        pl.BlockSpec((seq, 512), lambda b, i: (b, 0)),
        pl.BlockSpec((nk, DIFF_HEADS, DIFF_V, BK), lambda b, i: (b, 0, 0, 0)),
        _const_spec(bias.shape),
        _const_spec((1, DIFF_QK)), _const_spec((1, DIFF_QK)), _const_spec((1, DIFF_QK)), _const_spec((1, DIFF_QK)),
        _const_spec((DIFF_V, 1)),
    ]
    out_specs = (pl.BlockSpec((BQ, MLA_HEADS * LANES), qrow), pl.BlockSpec((BQ, DIFF_HEADS * DIFF_V), qrow))
    out_shape = (jax.ShapeDtypeStruct((n, MLA_HEADS * LANES), BF16),
                 jax.ShapeDtypeStruct((n, DIFF_HEADS * DIFF_V), BF16))
    return pl.pallas_call(
        _attn_kernel,
        grid=(batch, nq),
        in_specs=in_specs,
        out_specs=out_specs,
        out_shape=out_shape,
        scratch_shapes=[
            pltpu.VMEM((2 * DIFF_HEADS, BQ, LANES), BF16),
            pltpu.VMEM((N_STATE, 1, BQ), F32),
            pltpu.VMEM((N_STATE, 1, BQ), F32),
            pltpu.VMEM((N_STATE, LANES, BQ), F32),
        ],
        compiler_params=_cparams(("parallel", "parallel")),
        name="attn_prompt",
    )(qcat, dq, kvcat, ckvt, dk16, dvt, bias, *lam_vecs, subg_col)


def _decode_kernel(pt_ref, q_ref, dq_ref, kvs_ref, dks_ref, dvs_ref, bd_ref, bs_ref,
                   lq1_ref, lk1_ref, lq2_ref, lk2_ref, subg_ref, *rest):
    np_ = DEC_PAGES
    ckv_refs = rest[0:np_]
    kpe_refs = rest[np_:2 * np_]
    dk_refs = rest[2 * np_:3 * np_]
    dv_refs = rest[3 * np_:4 * np_]
    olat_ref, odiff_ref, mm_scr, lm_scr, am_scr, md_scr, ld_scr, ad_scr = rest[4 * np_:]
    g = pl.program_id(1)
    nrow = 2 * DIFF_HEADS

    q = q_ref[0]
    qa = q[:, :MLA_KVR]
    qp = q[:, MLA_KVR:MLA_KVR + MLA_ROPE]
    row = lax.broadcasted_iota(jnp.int32, (nrow, 512), 0)
    lane = lax.broadcasted_iota(jnp.int32, (nrow, 512), 1)
    dq_b = jnp.broadcast_to(dq_ref[0].astype(F32), (nrow, 512))
    qbd = jnp.where((lane >> 6) == row, dq_b, 0.0).astype(BF16)

    @pl.when(g == 0)
    def _():
        kvs = kvs_ref[0].astype(F32)
        mm_scr[...] = jnp.sum(q.astype(F32) * kvs, axis=1, keepdims=True)
        lm_scr[...] = jnp.ones(lm_scr.shape, F32)
        am_scr[...] = jnp.broadcast_to(kvs[:, :MLA_KVR], am_scr.shape)
        sd = jnp.sum(qbd.astype(F32) * dks_ref[0].astype(F32), axis=1, keepdims=True)
        md_scr[...] = sd + bs_ref[:, 0:1]
        ld_scr[...] = jnp.ones(ld_scr.shape, F32)
        ad_scr[...] = jnp.broadcast_to(dvs_ref[0].astype(F32), ad_scr.shape)

    tdims = (((1,), (1,)), ((), ()))
    ckv16 = [r[...].astype(BF16) for r in ckv_refs]
    sm = jnp.concatenate(
        [lax.dot_general(qa, ckv16[k], tdims, preferred_element_type=F32)
         + lax.dot_general(qp, kpe_refs[k][...].astype(BF16), tdims, preferred_element_type=F32)
         for k in range(np_)], axis=1)
    sd = jnp.concatenate(
        [lax.dot_general(qbd, dk_refs[k][...].astype(BF16), tdims, preferred_element_type=F32)
         for k in range(np_)], axis=1) + bd_ref[...]

    def merge(s, m_scr, l_scr):
        m_old = m_scr[...]
        m_new = jnp.maximum(m_old, jnp.max(s, axis=1, keepdims=True))
        a = jnp.exp(m_old - m_new)
        p = jnp.exp(s - m_new)
        l_scr[...] = l_scr[...] * a + jnp.sum(p, axis=1, keepdims=True)
        m_scr[...] = m_new
        return a, p.astype(BF16)

    a, p = merge(sm, mm_scr, lm_scr)
    acc = am_scr[...] * a
    for k in range(np_):
        acc = acc + jnp.dot(p[:, k * PAGE:(k + 1) * PAGE], ckv16[k], preferred_element_type=F32)
    am_scr[...] = acc

    a, p = merge(sd, md_scr, ld_scr)
    acc = ad_scr[...] * a
    for k in range(np_):
        acc = acc + jnp.dot(p[:, k * PAGE:(k + 1) * PAGE], dv_refs[k][...].astype(BF16),
                            preferred_element_type=F32)
    ad_scr[...] = acc

    @pl.when(g == pl.num_programs(1) - 1)
    def _():
        olat_ref[0] = am_scr[...] / lm_scr[...]
        lam = _lambda(lq1_ref, lk1_ref, lq2_ref, lk2_ref)
        o = ad_scr[...] / ld_scr[...]
        for hd in range(DIFF_HEADS):
            o1 = o[2 * hd:2 * hd + 1, hd * DIFF_V:(hd + 1) * DIFF_V]
            o2 = o[2 * hd + 1:2 * hd + 2, hd * DIFF_V:(hd + 1) * DIFF_V]
            od = o1 - lam * o2
            y = _rms(od, subg_ref[...], SUBLN_EPS) * (1.0 - LAM_INIT)
            odiff_ref[0, hd:hd + 1, :] = y


def _attn_decode(page_table, qcat, dq, kvcat, dk16, dv16, cache_ckv, cache_kpe, cache_dk, cache_dv,
                 bias_dec, bias_self, lam_vecs, subg_row):
    nb, n_pages = page_table.shape
    np_ = DEC_PAGES
    ng = n_pages // np_
    pt = page_table.reshape(-1).astype(jnp.int32)

    def page_spec(width, k):
        return pl.BlockSpec((None, PAGE, width), lambda b, g, pt: (pt[b * n_pages + g * np_ + k], 0, 0))

    brow3 = lambda b, g, pt: (b, 0, 0)
    c2 = lambda b, g, pt: (0, 0)
    in_specs = [
        pl.BlockSpec((1, MLA_HEADS, QCAT), brow3),
        pl.BlockSpec((1, 1, 512), brow3),
        pl.BlockSpec((1, 1, QCAT), brow3),
        pl.BlockSpec((1, 1, 512), brow3),
        pl.BlockSpec((1, 1, 512), brow3),
        pl.BlockSpec((2 * DIFF_HEADS, PAGE * np_), lambda b, g, pt: (0, g)),
        pl.BlockSpec((2 * DIFF_HEADS, LANES), c2),
        pl.BlockSpec((1, DIFF_QK), c2), pl.BlockSpec((1, DIFF_QK), c2),
        pl.BlockSpec((1, DIFF_QK), c2), pl.BlockSpec((1, DIFF_QK), c2),
        pl.BlockSpec((1, DIFF_V), c2),
    ]
    in_specs += [page_spec(MLA_KVR, k) for k in range(np_)]
    in_specs += [page_spec(MLA_ROPE, k) for k in range(np_)]
    in_specs += [page_spec(512, k) for k in range(np_)]
    in_specs += [page_spec(512, k) for k in range(np_)]
    out_specs = (pl.BlockSpec((1, MLA_HEADS, LANES), brow3), pl.BlockSpec((1, DIFF_HEADS, DIFF_V), brow3))
    out_shape = (jax.ShapeDtypeStruct((nb, MLA_HEADS, LANES), F32),
                 jax.ShapeDtypeStruct((nb, DIFF_HEADS, DIFF_V), F32))
    nrow = 2 * DIFF_HEADS
    grid_spec = pltpu.PrefetchScalarGridSpec(
        num_scalar_prefetch=1,
        grid=(nb, ng),
        in_specs=in_specs,
        out_specs=out_specs,
        scratch_shapes=[
            pltpu.VMEM((MLA_HEADS, 1), F32), pltpu.VMEM((MLA_HEADS, 1), F32), pltpu.VMEM((MLA_HEADS, MLA_KVR), F32),
            pltpu.VMEM((nrow, 1), F32), pltpu.VMEM((nrow, 1), F32), pltpu.VMEM((nrow, 512), F32),
        ],
    )
    olat, odiff = pl.pallas_call(
        _decode_kernel,
        grid_spec=grid_spec,
        out_shape=out_shape,
        compiler_params=_cparams(("parallel", "arbitrary")),
        name="attn_decode",
    )(pt, qcat.reshape(nb, MLA_HEADS, QCAT), dq.reshape(nb, 1, 512), kvcat.reshape(nb, 1, QCAT),
      dk16.reshape(nb, 1, 512), dv16.reshape(nb, 1, 512), bias_dec, bias_self, *lam_vecs, subg_row,
      *([cache_ckv] * np_), *([cache_kpe] * np_), *([cache_dk] * np_), *([cache_dv] * np_))
    return (olat.reshape(nb, MLA_HEADS * LANES).astype(BF16),
            odiff.reshape(nb, DIFF_HEADS * DIFF_V).astype(BF16))


def _post_kernel(x_ref, olat_ref, odiff_ref, ga_ref, gb_ref, wf2_ref, wbd_ref, wo_ref,
                 gffn_ref, wg_ref, wu_ref, wd_ref, gfin_ref, y_ref):
    bm = jnp.dot(olat_ref[...], wf2_ref[...], preferred_element_type=F32)
    bd = jnp.dot(odiff_ref[...], wbd_ref[...], preferred_element_type=F32)
    merged = (ga_ref[...] * bm + gb_ref[...] * bd).astype(BF16)
    x = x_ref[...] + jnp.dot(merged, wo_ref[...], preferred_element_type=F32)
    h2 = _rms(x, gffn_ref[...], NORM_EPS).astype(BF16)
    gate = jnp.dot(h2, wg_ref[...], preferred_element_type=F32)
    up = jnp.dot(h2, wu_ref[...], preferred_element_type=F32)
    act = (gate * jax.nn.sigmoid(gate) * up).astype(BF16)
    x = x + jnp.dot(act, wd_ref[...], preferred_element_type=F32)
    y_ref[...] = _rms(x, gfin_ref[...], NORM_EPS)


def _post(x, olat, odiff, ga, gb, weights, tm):
    n = x.shape[0]
    row = lambda i: (i, 0)
    wf2, wbd, wo, gffn, wg, wu, wd, gfin = weights
    single = pl.Buffered(1)

    def wspec(shape):
        return pl.BlockSpec(shape, lambda i: (0, 0), pipeline_mode=single)

    in_specs = [
        pl.BlockSpec((tm, D_MODEL), row),
        pl.BlockSpec((tm, MLA_HEADS * LANES), row),
        pl.BlockSpec((tm, 512), row),
        pl.BlockSpec((tm, D_MODEL), row),
        pl.BlockSpec((tm, D_MODEL), row),
        wspec(wf2.shape), wspec(wbd.shape), wspec(wo.shape), wspec(gffn.shape),
        wspec(wg.shape), wspec(wu.shape), wspec(wd.shape), wspec(gfin.shape),
    ]
    return pl.pallas_call(
        _post_kernel,
        grid=(n // tm,),
        in_specs=in_specs,
        out_specs=pl.BlockSpec((tm, D_MODEL), row),
        out_shape=jax.ShapeDtypeStruct((n, D_MODEL), F32),
        compiler_params=_cparams(("parallel",)),
        name="merge_ffn",
    )(x, olat, odiff, ga, gb, wf2, wbd, wo, gffn, wg, wu, wd, gfin)


def kernel(x_prompt, x_sample, cache_mla_ckv, cache_mla_kpe, cache_diff_k, cache_diff_v, page_table, rel_bias_table, norm_attn_g, w_in, mla_q_norm_g, mla_w_uq, mla_kv_norm_g, mla_w_uk, mla_w_uv, diff_lambda_q1, diff_lambda_k1, diff_lambda_q2, diff_lambda_k2, diff_subln_g, w_branch_mla, w_branch_diff, w_o, norm_ffn_g, w_ffn_gate, w_ffn_up, w_ffn_down, norm_final_g):
    batch, seq, _ = x_prompt.shape
    nb, dec_seq, _ = x_sample.shape
    assert dec_seq == 1 and w_in.shape[0] == 1
    n_pages = page_table.shape[1]
    past_len = n_pages * PAGE
    assert REL_MAX_DIST <= BQ + 1 and seq % BQ == 0 and n_pages % DEC_PAGES == 0
    n_pool = cache_mla_ckv.shape[1]

    w = w_in[0]
    sw = np.concatenate([np.arange(16, 32), np.arange(0, 16)])
    z96 = jnp.zeros((D_MODEL, LANES - MLA_ROPE), F32)
    w_kpe = w[:, 512:544]
    w_in_r = jnp.concatenate(
        [w[:, 0:384], w[:, 384:512], w_kpe, z96, w_kpe[:, sw], z96,
         w[:, 544:1056] * DIFF_SCALE, w[:, 1056:1568], w[:, 1568:2080], w[:, 2080:3104], w[:, 3104:4128]],
        axis=1).astype(BF16)
    w_fold, w_f2 = _fold_weights(mla_w_uq[0], mla_w_uk[0], mla_w_uv[0], w_branch_mla[0])
    uq = mla_w_uq[0].reshape(MLA_QR, MLA_HEADS, MLA_NOPE + MLA_ROPE)
    w_pe = uq[:, :, MLA_NOPE:]
    w_q2 = jnp.concatenate([w_fold, w_pe.reshape(MLA_QR, -1), w_pe[:, :, sw].reshape(MLA_QR, -1)],
                           axis=1).astype(BF16)
    proj_w = (norm_attn_g[0].reshape(1, -1), w_in_r, mla_q_norm_g[0].reshape(1, -1), w_q2,
              mla_kv_norm_g[0].reshape(1, -1))
    post_w = (w_f2.astype(BF16), w_branch_diff[0].astype(BF16), w_o[0].astype(BF16),
              norm_ffn_g[0].reshape(1, -1), w_ffn_gate[0].astype(BF16), w_ffn_up[0].astype(BF16),
              w_ffn_down[0].astype(BF16), norm_final_g.reshape(1, -1))
    lam_vecs = tuple(v[0].reshape(1, -1) for v in (diff_lambda_q1, diff_lambda_k1, diff_lambda_q2, diff_lambda_k2))
    subg = diff_subln_g[0]

    tab_p = _rope_tables(jnp.arange(seq, dtype=jnp.int32))
    tab_s = _rope_tables(jnp.full((nb,), past_len, dtype=jnp.int32))
    bias_p, bias_dec, bias_self = _bias_tables(rel_bias_table, past_len)

    xp = x_prompt.reshape(batch * seq, D_MODEL)
    (p_ckv, p_kpe, p_dk, p_dv, p_kvcat, p_ckvt, p_qcat, p_dq, p_dk16, p_dvt, p_ga, p_gb) = _proj(
        xp, tab_p, BK, proj_w)
    nk = seq // BK
    p_olat, p_odiff = _attn_prompt(
        p_qcat, p_dq, p_kvcat, p_ckvt, p_dk16, p_dvt, bias_p, lam_vecs, subg.reshape(-1, 1), batch, seq)
    y_prompt = _post(xp, p_olat, p_odiff, p_ga, p_gb, post_w, 256).reshape(batch, seq, D_MODEL)

    xs = x_sample.reshape(nb, D_MODEL)
    (s_ckv, s_kpe, s_dk, s_dv, s_kvcat, _, s_qcat, s_dq, s_dk16, s_dvt, s_ga, s_gb) = _proj(
        xs, tab_s, nb, proj_w)
    s_dv16 = s_dv.astype(BF16)
    s_olat, s_odiff = _attn_decode(
        page_table, s_qcat, s_dq, s_kvcat, s_dk16, s_dv16,
        cache_mla_ckv.reshape(n_pool, PAGE, MLA_KVR), cache_mla_kpe.reshape(n_pool, PAGE, MLA_ROPE),
        cache_diff_k.reshape(n_pool, PAGE, 512), cache_diff_v.reshape(n_pool, PAGE, 512),
        bias_dec, bias_self, lam_vecs, subg.reshape(1, -1))
    y_sample = _post(xs, s_olat, s_odiff, s_ga, s_gb, post_w, nb).reshape(nb, 1, D_MODEL)

    return (y_prompt, y_sample,
            p_ckv.reshape(1, batch, seq, MLA_KVR), p_kpe.reshape(1, batch, seq, MLA_ROPE),
            p_dk.reshape(1, batch, seq, DIFF_HEADS, 2, DIFF_QK), p_dv.reshape(1, batch, seq, DIFF_HEADS, DIFF_V),
            s_ckv.reshape(1, nb, 1, MLA_KVR), s_kpe.reshape(1, nb, 1, MLA_ROPE),
            s_dk.reshape(1, nb, 1, DIFF_HEADS, 2, DIFF_QK), s_dv.reshape(1, nb, 1, DIFF_HEADS, DIFF_V))
```

```python
import functools
import math

import numpy as np
import jax
import jax.numpy as jnp
from jax import lax
from jax.experimental import pallas as pl
from jax.experimental.pallas import tpu as pltpu

F32 = jnp.float32
BF16 = jnp.bfloat16

D_MODEL = 1024
PAGE = 128
MLA_HEADS = 8
MLA_NOPE = 64
MLA_ROPE = 32
MLA_V = 64
MLA_QR = 384
MLA_KVR = 128
MLA_SCALE = (MLA_NOPE + MLA_ROPE) ** -0.5
ROPE_THETA = 10000.0
DIFF_HEADS = 4
DIFF_QK = 64
DIFF_V = 128
DIFF_SCALE = DIFF_QK ** -0.5
SUBLN_EPS = 1e-5
REL_BUCKETS = 32
REL_MAX_DIST = 128
FFN_DIM = 2816
NORM_EPS = 1e-6
LAM_INIT = 0.8 - 0.6 * math.exp(-0.3 * 0)

LANES = 128
QCAT = 2 * LANES
NEG = -1e30
LOG2E = math.log2(math.e)
VMEM_LIMIT = 56 * 1024 * 1024

C_QLAT = 0
C_KV = 384
C_KPE = 512
C_KPESW = 640
C_DQ = 768
C_DK = 1280
C_DV = 1792
C_GA = 2304
C_GB = 3328
C_END = 4352

BQ = 256
BK = 256
DEC_PAGES = 8


def _cparams(sem):
    return pltpu.CompilerParams(dimension_semantics=sem, vmem_limit_bytes=VMEM_LIMIT)


def _const_spec(shape):
    nd = len(shape)
    return pl.BlockSpec(shape, lambda *_: (0,) * nd, pipeline_mode=pl.Buffered(1))


def _rope_tables_kernel(pos_ref, inv_ref, cq_ref, sq_ref, ck_ref, sk_ref):
    ang = pos_ref[...] * inv_ref[...]
    c = jnp.cos(ang)
    s = jnp.sin(ang)
    lane = lax.broadcasted_iota(jnp.int32, ang.shape, 1)
    first_half = (lane & (MLA_ROPE - 1)) < (MLA_ROPE // 2)
    s_signed = jnp.where(first_half, -s, s)
    cq_ref[...] = c
    sq_ref[...] = s_signed
    kmask = lax.broadcasted_iota(jnp.int32, ck_ref.shape, 1) < MLA_ROPE
    ck_ref[...] = jnp.where(kmask, c[:, :LANES], 0.0)
    sk_ref[...] = jnp.where(kmask, s_signed[:, :LANES], 0.0)


def _rope_tables(pos):
    n = pos.shape[0]
    half = MLA_ROPE // 2
    inv = (ROPE_THETA ** (-jnp.arange(half, dtype=F32) / half))
    inv = jnp.tile(inv, 2 * MLA_HEADS).reshape(1, 2 * LANES)
    return pl.pallas_call(
        _rope_tables_kernel,
        out_shape=(jax.ShapeDtypeStruct((n, 2 * LANES), F32), jax.ShapeDtypeStruct((n, 2 * LANES), F32),
                   jax.ShapeDtypeStruct((n, LANES), F32), jax.ShapeDtypeStruct((n, LANES), F32)),
        name="rope_tables",
    )(pos.astype(F32).reshape(n, 1), inv)


def _bucket(n):
    max_exact = REL_BUCKETS // 2
    nf = jnp.maximum(n, 1).astype(F32)
    large = max_exact + (jnp.log(nf / max_exact) / math.log(REL_MAX_DIST / max_exact)
                         * (REL_BUCKETS - max_exact)).astype(jnp.int32)
    large = jnp.minimum(large, REL_BUCKETS - 1)
    return jnp.where(n < max_exact, n, large)


def _bias_kernel(tab_ref, bp_ref, bd_ref, bs_ref, *, past_len):
    kidx = lax.broadcasted_iota(jnp.int32, (BK, BQ), 0)
    qidx = lax.broadcasted_iota(jnp.int32, (BK, BQ), 1)
    for d in range(3):
        rel = d * BQ + qidx - kidx
        b = _bucket(jnp.maximum(rel, 0))
        for h in range(DIFF_HEADS):
            val = jnp.zeros((BK, BQ), F32)
            for t in range(REL_BUCKETS):
                val = jnp.where(b == t, tab_ref[t, h] * LOG2E, val)
            if d == 0:
                val = jnp.where(rel >= 0, val, NEG)
            bp_ref[d * DIFF_HEADS + h] = val
    bp_ref[3 * DIFF_HEADS] = jnp.where(qidx - kidx >= 0, 0.0, NEG)
    kpos = lax.broadcasted_iota(jnp.int32, (2 * DIFF_HEADS, past_len), 1)
    row = lax.broadcasted_iota(jnp.int32, (2 * DIFF_HEADS, past_len), 0)
    b = _bucket(jnp.maximum(past_len - kpos, 0))
    val = jnp.zeros((2 * DIFF_HEADS, past_len), F32)
    for h in range(DIFF_HEADS):
        for t in range(REL_BUCKETS):
            val = jnp.where((b == t) & ((row >> 1) == h), tab_ref[t, h] * LOG2E, val)
    bd_ref[...] = val
    rows = lax.broadcasted_iota(jnp.int32, (2 * DIFF_HEADS, LANES), 0)
    b0 = _bucket(jnp.zeros((2 * DIFF_HEADS, LANES), jnp.int32))
    vs = jnp.zeros((2 * DIFF_HEADS, LANES), F32)
    for h in range(DIFF_HEADS):
        for t in range(REL_BUCKETS):
            vs = jnp.where((b0 == t) & ((rows >> 1) == h), tab_ref[t, h] * LOG2E, vs)
    bs_ref[...] = vs


def _bias_tables(table, past_len):
    return pl.pallas_call(
        functools.partial(_bias_kernel, past_len=past_len),
        in_specs=[pl.BlockSpec(memory_space=pltpu.SMEM)],
        out_shape=(jax.ShapeDtypeStruct((3 * DIFF_HEADS + 1, BK, BQ), F32),
                   jax.ShapeDtypeStruct((2 * DIFF_HEADS, past_len), F32),
                   jax.ShapeDtypeStruct((2 * DIFF_HEADS, LANES), F32)),
        compiler_params=pltpu.CompilerParams(vmem_limit_bytes=VMEM_LIMIT),
        name="bias_tables",
    )(table.astype(F32))


def _fold_kernel(nope_ref, uk_ref, uv_ref, bm_ref, wfold_ref, wf2_ref):
    for h in range(MLA_HEADS):
        wfold_ref[:, h * LANES:(h + 1) * LANES] = lax.dot_general(
            nope_ref[h], uk_ref[h], (((1,), (1,)), ((), ())),
            preferred_element_type=F32, precision=lax.Precision.HIGHEST)
        wf2_ref[h * LANES:(h + 1) * LANES, :] = jnp.dot(
            uv_ref[h], bm_ref[h], preferred_element_type=F32, precision=lax.Precision.HIGHEST)


def _fold_weights(w_uq, w_uk, w_uv, w_bm):
    uq = w_uq.reshape(MLA_QR, MLA_HEADS, MLA_NOPE + MLA_ROPE)
    nope = jnp.transpose(uq[:, :, :MLA_NOPE], (1, 0, 2))
    uk = jnp.transpose(w_uk, (1, 0, 2))
    uv = jnp.transpose(w_uv, (1, 0, 2))
    bm = w_bm.reshape(MLA_HEADS, MLA_V, D_MODEL)
    return pl.pallas_call(
        _fold_kernel,
        out_shape=(jax.ShapeDtypeStruct((MLA_QR, MLA_HEADS * LANES), F32),
                   jax.ShapeDtypeStruct((MLA_HEADS * LANES, D_MODEL), F32)),
        compiler_params=pltpu.CompilerParams(vmem_limit_bytes=VMEM_LIMIT),
        name="fold_weights",
    )(nope, uk, uv, bm)


def _rms(x, g, eps):
    return x * lax.rsqrt(jnp.mean(x * x, axis=-1, keepdims=True) + eps) * g


def _proj_kernel(x_ref, g_ref, win_ref, gq_ref, wq2_ref, gkv_ref, cq_ref, sq_ref, ck_ref, sk_ref,
                 ckv_ref, kpet_ref, dkt_ref, dv_ref, kvcat_ref, ckvt_ref, qcat_ref, dq_ref, dk16_ref, dvt_ref,
                 ga_ref, gb_ref):
    tm = x_ref.shape[0]
    h = _rms(x_ref[...], g_ref[...], NORM_EPS).astype(BF16)

    def seg(lo, hi):
        return jnp.dot(h, win_ref[:, lo:hi], preferred_element_type=F32)

    qn = _rms(seg(C_QLAT, C_KV), gq_ref[...], NORM_EPS).astype(BF16)
    zq = jnp.dot(qn, wq2_ref[...], preferred_element_type=F32)
    nq = MLA_HEADS * LANES
    qscale = MLA_SCALE * LOG2E
    qpe = (zq[:, nq:nq + 2 * LANES] * cq_ref[...] + zq[:, nq + 2 * LANES:] * sq_ref[...]) * qscale
    lane = lax.broadcasted_iota(jnp.int32, (tm, LANES), 1)
    per_block = LANES // MLA_ROPE
    for hd in range(MLA_HEADS):
        blk = qpe[:, (hd // per_block) * LANES:(hd // per_block + 1) * LANES]
        sh = (hd % per_block) * MLA_ROPE
        if sh:
            blk = pltpu.roll(blk, LANES - sh, axis=1)
        blk = jnp.where(lane < MLA_ROPE, blk, 0.0)
        qcat_ref[:, hd * QCAT:hd * QCAT + LANES] = (zq[:, hd * LANES:(hd + 1) * LANES] * qscale).astype(BF16)
        qcat_ref[:, hd * QCAT + LANES:(hd + 1) * QCAT] = blk.astype(BF16)

    ckv = _rms(seg(C_KV, C_KPE), gkv_ref[...], NORM_EPS)
    kpe = seg(C_KPE, C_KPESW) * ck_ref[...] + seg(C_KPESW, C_DQ) * sk_ref[...]
    ckv_ref[...] = ckv
    kpet_ref[0] = kpe.T[:MLA_ROPE]
    kvcat_ref[:, :LANES] = ckv.astype(BF16)
    kvcat_ref[:, LANES:] = kpe.astype(BF16)
    ckvt_ref[0] = ckv.T.astype(BF16)

    dq_ref[...] = (seg(C_DQ, C_DK) * LOG2E).astype(BF16)
    dk = seg(C_DK, C_DV)
    dkt_ref[0] = dk.T
    dk16_ref[...] = dk.astype(BF16)
    dv = seg(C_DV, C_GA)
    dv_ref[...] = dv
    for hd in range(DIFF_HEADS):
        dvt_ref[0, hd] = dv[:, hd * DIFF_V:(hd + 1) * DIFF_V].T.astype(BF16)

    ga_ref[...] = jax.nn.sigmoid(seg(C_GA, C_GB))
    gb_ref[...] = jax.nn.sigmoid(seg(C_GB, C_END))


def _proj(x, tables, tm, weights):
    g_attn, w_in_r, g_q, w_q2, g_kv = weights
    cq, sq, ck, sk = tables
    n = x.shape[0]
    nb = n // tm
    seq = cq.shape[0]
    tb = seq // tm
    batch = n // seq
    tcol = lambda i: (i // tb, 0, i % tb)
    row = lambda i: (i, 0)
    trow = lambda i: (i % tb, 0)
    in_specs = [
        pl.BlockSpec((tm, D_MODEL), row),
        _const_spec((1, D_MODEL)),
        _const_spec((D_MODEL, C_END)),
        _const_spec((1, MLA_QR)),
        _const_spec((MLA_QR, w_q2.shape[1])),
        _const_spec((1, MLA_KVR)),
        pl.BlockSpec((tm, 2 * LANES), trow), pl.BlockSpec((tm, 2 * LANES), trow),
        pl.BlockSpec((tm, LANES), trow), pl.BlockSpec((tm, LANES), trow),
    ]
    out_shape = (
        jax.ShapeDtypeStruct((n, MLA_KVR), F32),
        jax.ShapeDtypeStruct((batch, MLA_ROPE, seq), F32),
        jax.ShapeDtypeStruct((batch, 2 * DIFF_HEADS * DIFF_QK, seq), F32),
        jax.ShapeDtypeStruct((n, DIFF_HEADS * DIFF_V), F32),
        jax.ShapeDtypeStruct((n, QCAT), BF16),
        jax.ShapeDtypeStruct((nb, MLA_KVR, tm), BF16),
        jax.ShapeDtypeStruct((n, MLA_HEADS * QCAT), BF16),
        jax.ShapeDtypeStruct((n, 2 * DIFF_HEADS * DIFF_QK), BF16),
        jax.ShapeDtypeStruct((n, 2 * DIFF_HEADS * DIFF_QK), BF16),
        jax.ShapeDtypeStruct((nb, DIFF_HEADS, DIFF_V, tm), BF16),
        jax.ShapeDtypeStruct((n, D_MODEL), F32),
        jax.ShapeDtypeStruct((n, D_MODEL), F32),
    )
    out_specs = (
        pl.BlockSpec((tm, MLA_KVR), row),
        pl.BlockSpec((1, MLA_ROPE, tm), tcol),
        pl.BlockSpec((1, 512, tm), tcol),
        pl.BlockSpec((tm, 512), row),
        pl.BlockSpec((tm, QCAT), row),
        pl.BlockSpec((1, MLA_KVR, tm), lambda i: (i, 0, 0)),
        pl.BlockSpec((tm, MLA_HEADS * QCAT), row),
        pl.BlockSpec((tm, 512), row),
        pl.BlockSpec((tm, 512), row),
        pl.BlockSpec((1, DIFF_HEADS, DIFF_V, tm), lambda i: (i, 0, 0, 0)),
        pl.BlockSpec((tm, D_MODEL), row),
        pl.BlockSpec((tm, D_MODEL), row),
    )
    return pl.pallas_call(
        _proj_kernel,
        grid=(nb,),
        in_specs=in_specs,
        out_specs=out_specs,
        out_shape=out_shape,
        compiler_params=_cparams(("parallel",)),
        name="proj_in",
    )(x, g_attn, w_in_r, g_q, w_q2, g_kv, cq, sq, ck, sk)


def _lambda(lq1_ref, lk1_ref, lq2_ref, lk2_ref):
    a = jnp.sum(lq1_ref[...] * lk1_ref[...], axis=-1, keepdims=True)
    b = jnp.sum(lq2_ref[...] * lk2_ref[...], axis=-1, keepdims=True)
    return jnp.exp(a) - jnp.exp(b) + LAM_INIT


NQ_MLA = MLA_HEADS * BQ
NQ_DIFF = 2 * BQ


def _attn_kernel(qcat_ref, dq_ref, kv_ref, ckvt_ref, dk_ref, dvt_ref, bias_ref,
                 lq1_ref, lk1_ref, lq2_ref, lk2_ref, subg_ref,
                 olat_ref, odiff_ref, qall_scr, qd_scr, mm_scr, lm_scr, am_scr, md_scr, ld_scr, ad_scr):
    i = pl.program_id(1)

    for hd in range(MLA_HEADS):
        qall_scr[hd * BQ:(hd + 1) * BQ, :] = qcat_ref[:, hd * QCAT:(hd + 1) * QCAT]
    lane = lax.broadcasted_iota(jnp.int32, (BQ, LANES), 1)
    for hd in range(DIFF_HEADS):
        dqh = dq_ref[:, hd * LANES:(hd + 1) * LANES]
        qd_scr[hd, 0:BQ, :] = jnp.where(lane < DIFF_QK, dqh, jnp.zeros_like(dqh))
        qd_scr[hd, BQ:2 * BQ, :] = jnp.where(lane >= DIFF_QK, dqh, jnp.zeros_like(dqh))

    mm_scr[...] = jnp.full(mm_scr.shape, NEG, F32)
    lm_scr[...] = jnp.zeros(lm_scr.shape, F32)
    am_scr[...] = jnp.zeros(am_scr.shape, F32)
    md_scr[...] = jnp.full(md_scr.shape, NEG, F32)
    ld_scr[...] = jnp.zeros(ld_scr.shape, F32)
    ad_scr[...] = jnp.zeros(ad_scr.shape, F32)

    def update(m_ref, l_ref, acc_ref, k_op, q_op, vt_op, bias):
        st = lax.dot_general(k_op, q_op, (((1,), (1,)), ((), ())), preferred_element_type=F32)
        if bias is not None:
            st = st + bias
        m_old = m_ref[...]
        m_new = jnp.maximum(m_old, jnp.max(st, axis=0, keepdims=True))
        a = jnp.exp2(m_old - m_new)
        pt = jnp.exp2(st - m_new)
        l_ref[...] = l_ref[...] * a + jnp.sum(pt, axis=0, keepdims=True)
        acc_ref[...] = acc_ref[...] * a + jnp.dot(vt_op, pt.astype(BF16), preferred_element_type=F32)
        m_ref[...] = m_new

    def block(j, diag):
        koff = pl.multiple_of(j * BK, BK)
        bias = None
        if diag:
            mask = bias_ref[3 * DIFF_HEADS]
            bias = jnp.concatenate([mask] * MLA_HEADS, axis=1)
        update(mm_scr, lm_scr, am_scr, kv_ref[pl.ds(koff, BK), :], qall_scr[...], ckvt_ref[j], bias)
        d = 0 if diag else jnp.minimum(i - j, 2)
        for hd in range(DIFF_HEADS):
            kb = dk_ref[pl.ds(koff, BK), hd * LANES:(hd + 1) * LANES]
            b1 = bias_ref[d * DIFF_HEADS + hd]
            update(md_scr.at[hd], ld_scr.at[hd], ad_scr.at[hd], kb, qd_scr[hd], dvt_ref[j, hd],
                   jnp.concatenate([b1, b1], axis=1))

    def body(j, carry):
        block(j, False)
        return carry

    lax.fori_loop(0, i, body, 0)
    block(i, True)

    o = am_scr[...] / lm_scr[...]
    for hd in range(MLA_HEADS):
        olat_ref[:, hd * LANES:(hd + 1) * LANES] = o[:, hd * BQ:(hd + 1) * BQ].T.astype(BF16)
    lam = _lambda(lq1_ref, lk1_ref, lq2_ref, lk2_ref)
    for hd in range(DIFF_HEADS):
        op = ad_scr[hd] / ld_scr[hd]
        od = op[:, :BQ] - lam * op[:, BQ:]
        y = od * lax.rsqrt(jnp.mean(od * od, axis=0, keepdims=True) + SUBLN_EPS) * subg_ref[...]
        odiff_ref[:, hd * LANES:(hd + 1) * LANES] = (y * (1.0 - LAM_INIT)).T.astype(BF16)


def _attn_prompt(qcat, dq, kvcat, ckvt, dk16, dvt, bias, lam_vecs, subg_col, batch, seq):
    nq = seq // BQ
    nk = seq // BK
    n = batch * seq
    qrow = lambda b, i: (b * nq + i, 0)
    in_specs = [
        pl.BlockSpec((BQ, MLA_HEADS * QCAT), qrow),
        pl.BlockSpec((BQ, 512), qrow),
        pl.BlockSpec((seq, QCAT), lambda b, i: (b, 0)),
        pl.BlockSpec((nk, MLA_KVR, BK), lambda b, i: (b, 0, 0)),
        pl.BlockSpec((seq, 512), lambda b, i: (b, 0)),
        pl.BlockSpec((nk, DIFF_HEADS, DIFF_V, BK), lambda b, i: (b, 0, 0, 0)),
        _const_spec(bias.shape),
        _const_spec((1, DIFF_QK)), _const_spec((1, DIFF_QK)), _const_spec((1, DIFF_QK)), _const_spec((1, DIFF_QK)),
        _const_spec((DIFF_V, 1)),
    ]
    out_specs = (pl.BlockSpec((BQ, MLA_HEADS * LANES), qrow), pl.BlockSpec((BQ, DIFF_HEADS * DIFF_V), qrow))
    out_shape = (jax.ShapeDtypeStruct((n, MLA_HEADS * LANES), BF16),
                 jax.ShapeDtypeStruct((n, DIFF_HEADS * DIFF_V), BF16))
    return pl.pallas_call(
        _attn_kernel,
        grid=(batch, nq),
        in_specs=in_specs,
        out_specs=out_specs,
        out_shape=out_shape,
        scratch_shapes=[
            pltpu.VMEM((NQ_MLA, QCAT), BF16),
            pltpu.VMEM((DIFF_HEADS, NQ_DIFF, LANES), BF16),
            pltpu.VMEM((1, NQ_MLA), F32), pltpu.VMEM((1, NQ_MLA), F32), pltpu.VMEM((MLA_KVR, NQ_MLA), F32),
            pltpu.VMEM((DIFF_HEADS, 1, NQ_DIFF), F32), pltpu.VMEM((DIFF_HEADS, 1, NQ_DIFF), F32),
            pltpu.VMEM((DIFF_HEADS, DIFF_V, NQ_DIFF), F32),
        ],
        compiler_params=_cparams(("parallel", "parallel")),
        name="attn_prompt",
    )(qcat, dq, kvcat, ckvt, dk16, dvt, bias, *lam_vecs, subg_col)


def _decode_kernel(pt_ref, q_ref, dq_ref, kvs_ref, dks_ref, dvs_ref, bd_ref, bs_ref,
                   lq1_ref, lk1_ref, lq2_ref, lk2_ref, subg_ref, *rest):
    np_ = DEC_PAGES
    ckv_refs = rest[0:np_]
    kpe_refs = rest[np_:2 * np_]
    dk_refs = rest[2 * np_:3 * np_]
    dv_refs = rest[3 * np_:4 * np_]
    olat_ref, odiff_ref, mm_scr, lm_scr, am_scr, md_scr, ld_scr, ad_scr = rest[4 * np_:]
    g = pl.program_id(1)
    nrow = 2 * DIFF_HEADS

    q = q_ref[0]
    qa = q[:, :MLA_KVR]
    qp = q[:, MLA_KVR:MLA_KVR + MLA_ROPE]
    row = lax.broadcasted_iota(jnp.int32, (nrow, 512), 0)
    lane = lax.broadcasted_iota(jnp.int32, (nrow, 512), 1)
    dq_b = jnp.broadcast_to(dq_ref[0].astype(F32), (nrow, 512))
    qbd = jnp.where((lane >> 6) == row, dq_b, 0.0).astype(BF16)

    @pl.when(g == 0)
    def _():
        kvs = kvs_ref[0].astype(F32)
        mm_scr[...] = jnp.sum(q.astype(F32) * kvs, axis=1, keepdims=True)
        lm_scr[...] = jnp.ones(lm_scr.shape, F32)
        am_scr[...] = jnp.broadcast_to(kvs[:, :MLA_KVR], am_scr.shape)
        sd = jnp.sum(qbd.astype(F32) * dks_ref[0].astype(F32), axis=1, keepdims=True)
        md_scr[...] = sd + bs_ref[:, 0:1]
        ld_scr[...] = jnp.ones(ld_scr.shape, F32)
        ad_scr[...] = jnp.broadcast_to(dvs_ref[0].astype(F32), ad_scr.shape)

    tdims = (((1,), (1,)), ((), ()))
    ckv16 = jnp.concatenate([r[...].astype(BF16) for r in ckv_refs], axis=0)
    kpet16 = jnp.concatenate([r[...].astype(BF16) for r in kpe_refs], axis=1)
    dkt16 = jnp.concatenate([r[...].astype(BF16) for r in dk_refs], axis=1)
    sm = (lax.dot_general(qa, ckv16, tdims, preferred_element_type=F32)
          + jnp.dot(qp, kpet16, preferred_element_type=F32))
    sd = jnp.dot(qbd, dkt16, preferred_element_type=F32) + bd_ref[...]

    def merge(s, m_scr, l_scr):
        m_old = m_scr[...]
        m_new = jnp.maximum(m_old, jnp.max(s, axis=1, keepdims=True))
        a = jnp.exp2(m_old - m_new)
        p = jnp.exp2(s - m_new)
        l_scr[...] = l_scr[...] * a + jnp.sum(p, axis=1, keepdims=True)
        m_scr[...] = m_new
        return a, p.astype(BF16)

    a, p = merge(sm, mm_scr, lm_scr)
    am_scr[...] = am_scr[...] * a + jnp.dot(p, ckv16, preferred_element_type=F32)

    a, p = merge(sd, md_scr, ld_scr)
    for hd in range(DIFF_HEADS):
        v_h = jnp.concatenate([r[pl.ds(hd, PAGE, stride=DIFF_HEADS), :].astype(BF16) for r in dv_refs],
                              axis=0)
        cols = slice(hd * DIFF_V, (hd + 1) * DIFF_V)
        ad_scr[:, cols] = ad_scr[:, cols] * a + jnp.dot(p, v_h, preferred_element_type=F32)

    @pl.when(g == pl.num_programs(1) - 1)
    def _():
        olat_ref[0] = am_scr[...] / lm_scr[...]
        lam = _lambda(lq1_ref, lk1_ref, lq2_ref, lk2_ref)
        o = ad_scr[...] / ld_scr[...]
        for hd in range(DIFF_HEADS):
            o1 = o[2 * hd:2 * hd + 1, hd * DIFF_V:(hd + 1) * DIFF_V]
            o2 = o[2 * hd + 1:2 * hd + 2, hd * DIFF_V:(hd + 1) * DIFF_V]
            od = o1 - lam * o2
            y = _rms(od, subg_ref[...], SUBLN_EPS) * (1.0 - LAM_INIT)
            odiff_ref[0, hd:hd + 1, :] = y


def _attn_decode(page_table, qcat, dq, kvcat, dk16, dv16, cache_ckv, cache_kpe, cache_dk, cache_dv,
                 bias_dec, bias_self, lam_vecs, subg_row):
    nb, n_pages = page_table.shape
    np_ = DEC_PAGES
    ng = n_pages // np_
    pt = page_table.reshape(-1).astype(jnp.int32)

    def page_spec(rows, k):
        return pl.BlockSpec((None, rows, PAGE), lambda b, g, pt: (pt[b * n_pages + g * np_ + k], 0, 0))

    brow3 = lambda b, g, pt: (b, 0, 0)
    c2 = lambda b, g, pt: (0, 0)
    in_specs = [
        pl.BlockSpec((1, MLA_HEADS, QCAT), brow3),
        pl.BlockSpec((1, 1, 512), brow3),
        pl.BlockSpec((1, 1, QCAT), brow3),
        pl.BlockSpec((1, 1, 512), brow3),
        pl.BlockSpec((1, 1, 512), brow3),
        pl.BlockSpec((2 * DIFF_HEADS, PAGE * np_), lambda b, g, pt: (0, g)),
        pl.BlockSpec((2 * DIFF_HEADS, LANES), c2),
        pl.BlockSpec((1, DIFF_QK), c2), pl.BlockSpec((1, DIFF_QK), c2),
        pl.BlockSpec((1, DIFF_QK), c2), pl.BlockSpec((1, DIFF_QK), c2),
        pl.BlockSpec((1, DIFF_V), c2),
    ]
    in_specs += [page_spec(PAGE, k) for k in range(np_)]
    in_specs += [page_spec(MLA_ROPE, k) for k in range(np_)]
    in_specs += [page_spec(512, k) for k in range(np_)]
    in_specs += [page_spec(PAGE * DIFF_HEADS, k) for k in range(np_)]
    out_specs = (pl.BlockSpec((1, MLA_HEADS, LANES), brow3), pl.BlockSpec((1, DIFF_HEADS, DIFF_V), brow3))
    out_shape = (jax.ShapeDtypeStruct((nb, MLA_HEADS, LANES), F32),
                 jax.ShapeDtypeStruct((nb, DIFF_HEADS, DIFF_V), F32))
    nrow = 2 * DIFF_HEADS
    grid_spec = pltpu.PrefetchScalarGridSpec(
        num_scalar_prefetch=1,
        grid=(nb, ng),
        in_specs=in_specs,
        out_specs=out_specs,
        scratch_shapes=[
            pltpu.VMEM((MLA_HEADS, 1), F32), pltpu.VMEM((MLA_HEADS, 1), F32), pltpu.VMEM((MLA_HEADS, MLA_KVR), F32),
            pltpu.VMEM((nrow, 1), F32), pltpu.VMEM((nrow, 1), F32), pltpu.VMEM((nrow, 512), F32),
        ],
    )
    olat, odiff = pl.pallas_call(
        _decode_kernel,
        grid_spec=grid_spec,
        out_shape=out_shape,
        compiler_params=_cparams(("parallel", "arbitrary")),
        name="attn_decode",
    )(pt, qcat.reshape(nb, MLA_HEADS, QCAT), dq.reshape(nb, 1, 512), kvcat.reshape(nb, 1, QCAT),
      dk16.reshape(nb, 1, 512), dv16.reshape(nb, 1, 512), bias_dec, bias_self, *lam_vecs, subg_row,
      *([cache_ckv] * np_), *([cache_kpe] * np_), *([cache_dk] * np_), *([cache_dv] * np_))
    return (olat.reshape(nb, MLA_HEADS * LANES).astype(BF16),
            odiff.reshape(nb, DIFF_HEADS * DIFF_V).astype(BF16))


def _post_kernel(x_ref, olat_ref, odiff_ref, ga_ref, gb_ref, wf2_ref, wbd_ref, wo_ref,
                 gffn_ref, wg_ref, wu_ref, wd_ref, gfin_ref, y_ref):
    bm = jnp.dot(olat_ref[...], wf2_ref[...], preferred_element_type=F32)
    bd = jnp.dot(odiff_ref[...], wbd_ref[...], preferred_element_type=F32)
    merged = (ga_ref[...] * bm + gb_ref[...] * bd).astype(BF16)
    x = x_ref[...] + jnp.dot(merged, wo_ref[...], preferred_element_type=F32)
    h2 = _rms(x, gffn_ref[...], NORM_EPS).astype(BF16)
    gate = jnp.dot(h2, wg_ref[...], preferred_element_type=F32)
    up = jnp.dot(h2, wu_ref[...], preferred_element_type=F32)
    act = (gate * jax.nn.sigmoid(gate) * up).astype(BF16)
    x = x + jnp.dot(act, wd_ref[...], preferred_element_type=F32)
    y_ref[...] = _rms(x, gfin_ref[...], NORM_EPS)


def _post(x, olat, odiff, ga, gb, weights, tm):
    n = x.shape[0]
    row = lambda i: (i, 0)
    wf2, wbd, wo, gffn, wg, wu, wd, gfin = weights
    single = pl.Buffered(1)

    def wspec(shape):
        return pl.BlockSpec(shape, lambda i: (0, 0), pipeline_mode=single)

    in_specs = [
        pl.BlockSpec((tm, D_MODEL), row),
        pl.BlockSpec((tm, MLA_HEADS * LANES), row),
        pl.BlockSpec((tm, 512), row),
        pl.BlockSpec((tm, D_MODEL), row),
        pl.BlockSpec((tm, D_MODEL), row),
        wspec(wf2.shape), wspec(wbd.shape), wspec(wo.shape), wspec(gffn.shape),
        wspec(wg.shape), wspec(wu.shape), wspec(wd.shape), wspec(gfin.shape),
    ]
    return pl.pallas_call(
        _post_kernel,
        grid=(n // tm,),
        in_specs=in_specs,
        out_specs=pl.BlockSpec((tm, D_MODEL), row),
        out_shape=jax.ShapeDtypeStruct((n, D_MODEL), F32),
        compiler_params=_cparams(("parallel",)),
        name="merge_ffn",
    )(x, olat, odiff, ga, gb, wf2, wbd, wo, gffn, wg, wu, wd, gfin)


def kernel(x_prompt, x_sample, cache_mla_ckv, cache_mla_kpe, cache_diff_k, cache_diff_v, page_table, rel_bias_table, norm_attn_g, w_in, mla_q_norm_g, mla_w_uq, mla_kv_norm_g, mla_w_uk, mla_w_uv, diff_lambda_q1, diff_lambda_k1, diff_lambda_q2, diff_lambda_k2, diff_subln_g, w_branch_mla, w_branch_diff, w_o, norm_ffn_g, w_ffn_gate, w_ffn_up, w_ffn_down, norm_final_g):
    batch, seq, _ = x_prompt.shape
    nb, dec_seq, _ = x_sample.shape
    assert dec_seq == 1 and w_in.shape[0] == 1
    n_pages = page_table.shape[1]
    past_len = n_pages * PAGE
    assert REL_MAX_DIST <= BQ + 1 and seq % BQ == 0 and n_pages % DEC_PAGES == 0
    n_pool = cache_mla_ckv.shape[1]

    w = w_in[0]
    sw = np.concatenate([np.arange(16, 32), np.arange(0, 16)])
    z96 = jnp.zeros((D_MODEL, LANES - MLA_ROPE), F32)
    w_kpe = w[:, 512:544]
    w_in_r = jnp.concatenate(
        [w[:, 0:384], w[:, 384:512], w_kpe, z96, w_kpe[:, sw], z96,
         w[:, 544:1056] * DIFF_SCALE, w[:, 1056:1568], w[:, 1568:2080], w[:, 2080:3104], w[:, 3104:4128]],
        axis=1).astype(BF16)
    w_fold, w_f2 = _fold_weights(mla_w_uq[0], mla_w_uk[0], mla_w_uv[0], w_branch_mla[0])
    uq = mla_w_uq[0].reshape(MLA_QR, MLA_HEADS, MLA_NOPE + MLA_ROPE)
    w_pe = uq[:, :, MLA_NOPE:]
    w_q2 = jnp.concatenate([w_fold, w_pe.reshape(MLA_QR, -1), w_pe[:, :, sw].reshape(MLA_QR, -1)],
                           axis=1).astype(BF16)
    proj_w = (norm_attn_g[0].reshape(1, -1), w_in_r, mla_q_norm_g[0].reshape(1, -1), w_q2,
              mla_kv_norm_g[0].reshape(1, -1))
    post_w = (w_f2.astype(BF16), w_branch_diff[0].astype(BF16), w_o[0].astype(BF16),
              norm_ffn_g[0].reshape(1, -1), w_ffn_gate[0].astype(BF16), w_ffn_up[0].astype(BF16),
              w_ffn_down[0].astype(BF16), norm_final_g.reshape(1, -1))
    lam_vecs = tuple(v[0].reshape(1, -1) for v in (diff_lambda_q1, diff_lambda_k1, diff_lambda_q2, diff_lambda_k2))
    subg = diff_subln_g[0]

    tab_p = _rope_tables(jnp.arange(seq, dtype=jnp.int32))
    tab_s = _rope_tables(jnp.full((nb,), past_len, dtype=jnp.int32))
    bias_p, bias_dec, bias_self = _bias_tables(rel_bias_table, past_len)

    xp = x_prompt.reshape(batch * seq, D_MODEL)
    (p_ckv, p_kpe, p_dk, p_dv, p_kvcat, p_ckvt, p_qcat, p_dq, p_dk16, p_dvt, p_ga, p_gb) = _proj(
        xp, tab_p, BK, proj_w)
    nk = seq // BK
    p_olat, p_odiff = _attn_prompt(
        p_qcat, p_dq, p_kvcat, p_ckvt, p_dk16, p_dvt, bias_p, lam_vecs, subg.reshape(-1, 1), batch, seq)
    y_prompt = _post(xp, p_olat, p_odiff, p_ga, p_gb, post_w, 256).reshape(batch, seq, D_MODEL)

    xs = x_sample.reshape(nb, D_MODEL)
    (s_ckv, s_kpe, s_dk, s_dv, s_kvcat, _, s_qcat, s_dq, s_dk16, s_dvt, s_ga, s_gb) = _proj(
        xs, tab_s, nb, proj_w)
    s_dv16 = s_dv.astype(BF16)
    s_olat, s_odiff = _attn_decode(
        page_table, s_qcat, s_dq, s_kvcat, s_dk16, s_dv16,
        cache_mla_ckv.reshape(n_pool, PAGE, MLA_KVR),
        jnp.transpose(cache_mla_kpe, (0, 1, 3, 2)).reshape(n_pool, MLA_ROPE, PAGE),
        jnp.transpose(cache_diff_k, (0, 1, 3, 4, 5, 2)).reshape(n_pool, 512, PAGE),
        cache_diff_v.reshape(n_pool, PAGE * DIFF_HEADS, DIFF_V),
        bias_dec, bias_self, lam_vecs, subg.reshape(1, -1))
    y_sample = _post(xs, s_olat, s_odiff, s_ga, s_gb, post_w, nb).reshape(nb, 1, D_MODEL)

    def kpe_out(t, b, s):
        return jnp.transpose(t, (0, 2, 1)).reshape(1, b, s, MLA_ROPE)

    def dk_out(t, b, s):
        t = t.reshape(b, DIFF_HEADS, 2, DIFF_QK, s)
        return jnp.transpose(t, (0, 4, 1, 2, 3)).reshape(1, b, s, DIFF_HEADS, 2, DIFF_QK)

    return (y_prompt, y_sample,
            p_ckv.reshape(1, batch, seq, MLA_KVR), kpe_out(p_kpe, batch, seq),
            dk_out(p_dk, batch, seq), p_dv.reshape(1, batch, seq, DIFF_HEADS, DIFF_V),
            s_ckv.reshape(1, nb, 1, MLA_KVR), kpe_out(s_kpe, 1, nb).reshape(1, nb, 1, MLA_ROPE),
            dk_out(s_dk, 1, nb).reshape(1, nb, 1, DIFF_HEADS, 2, DIFF_QK),
            s_dv.reshape(1, nb, 1, DIFF_HEADS, DIFF_V))
```

```python
import functools
import math

import numpy as np
import jax
import jax.numpy as jnp
from jax import lax
from jax.experimental import pallas as pl
from jax.experimental.pallas import tpu as pltpu

F32 = jnp.float32
BF16 = jnp.bfloat16

D_MODEL = 1024
PAGE = 128
MLA_HEADS = 8
MLA_NOPE = 64
MLA_ROPE = 32
MLA_V = 64
MLA_QR = 384
MLA_KVR = 128
MLA_SCALE = (MLA_NOPE + MLA_ROPE) ** -0.5
ROPE_THETA = 10000.0
DIFF_HEADS = 4
DIFF_QK = 64
DIFF_V = 128
DIFF_SCALE = DIFF_QK ** -0.5
SUBLN_EPS = 1e-5
REL_BUCKETS = 32
REL_MAX_DIST = 128
FFN_DIM = 2816
NORM_EPS = 1e-6
LAM_INIT = 0.8 - 0.6 * math.exp(-0.3 * 0)

LANES = 128
QCAT = 2 * LANES
NEG = -1e30
LOG2E = math.log2(math.e)
VMEM_LIMIT = 56 * 1024 * 1024

C_QLAT = 0
C_KV = 384
C_KPE = 512
C_KPESW = 640
C_DQ = 768
C_DK = 1280
C_DV = 1792
C_GA = 2304
C_GB = 3328
C_END = 4352

BQ = 256
BK = 256
DEC_PAGES = 8


def _cparams(sem):
    return pltpu.CompilerParams(dimension_semantics=sem, vmem_limit_bytes=VMEM_LIMIT)


def _const_spec(shape):
    nd = len(shape)
    return pl.BlockSpec(shape, lambda *_: (0,) * nd, pipeline_mode=pl.Buffered(1))


def _rope_tables_kernel(pos_ref, inv_ref, cq_ref, sq_ref, ck_ref, sk_ref):
    ang = pos_ref[...] * inv_ref[...]
    c = jnp.cos(ang)
    s = jnp.sin(ang)
    lane = lax.broadcasted_iota(jnp.int32, ang.shape, 1)
    first_half = (lane & (MLA_ROPE - 1)) < (MLA_ROPE // 2)
    s_signed = jnp.where(first_half, -s, s)
    cq_ref[...] = c
    sq_ref[...] = s_signed
    kmask = lax.broadcasted_iota(jnp.int32, ck_ref.shape, 1) < MLA_ROPE
    ck_ref[...] = jnp.where(kmask, c[:, :LANES], 0.0)
    sk_ref[...] = jnp.where(kmask, s_signed[:, :LANES], 0.0)


def _rope_tables(pos):
    n = pos.shape[0]
    half = MLA_ROPE // 2
    inv = (ROPE_THETA ** (-jnp.arange(half, dtype=F32) / half))
    inv = jnp.tile(inv, 2 * MLA_HEADS).reshape(1, 2 * LANES)
    return pl.pallas_call(
        _rope_tables_kernel,
        out_shape=(jax.ShapeDtypeStruct((n, 2 * LANES), F32), jax.ShapeDtypeStruct((n, 2 * LANES), F32),
                   jax.ShapeDtypeStruct((n, LANES), F32), jax.ShapeDtypeStruct((n, LANES), F32)),
        name="rope_tables",
    )(pos.astype(F32).reshape(n, 1), inv)


def _bucket(n):
    max_exact = REL_BUCKETS // 2
    nf = jnp.maximum(n, 1).astype(F32)
    large = max_exact + (jnp.log(nf / max_exact) / math.log(REL_MAX_DIST / max_exact)
                         * (REL_BUCKETS - max_exact)).astype(jnp.int32)
    large = jnp.minimum(large, REL_BUCKETS - 1)
    return jnp.where(n < max_exact, n, large)


def _bias_kernel(tab_ref, bp_ref, bd_ref, bs_ref, *, past_len):
    kidx = lax.broadcasted_iota(jnp.int32, (BK, BQ), 0)
    qidx = lax.broadcasted_iota(jnp.int32, (BK, BQ), 1)
    for d in range(3):
        rel = d * BQ + qidx - kidx
        b = _bucket(jnp.maximum(rel, 0))
        for h in range(DIFF_HEADS):
            val = jnp.zeros((BK, BQ), F32)
            for t in range(REL_BUCKETS):
                val = jnp.where(b == t, tab_ref[t, h] * LOG2E, val)
            if d == 0:
                val = jnp.where(rel >= 0, val, NEG)
            bp_ref[d * DIFF_HEADS + h] = val
    bp_ref[3 * DIFF_HEADS] = jnp.where(qidx - kidx >= 0, 0.0, NEG)
    gk = DEC_PAGES * PAGE
    row = lax.broadcasted_iota(jnp.int32, (2 * DIFF_HEADS, gk), 0)
    for g in range(past_len // gk):
        kpos = g * gk + lax.broadcasted_iota(jnp.int32, (2 * DIFF_HEADS, gk), 1)
        b = _bucket(jnp.maximum(past_len - kpos, 0))
        val = jnp.zeros((2 * DIFF_HEADS, gk), F32)
        for h in range(DIFF_HEADS):
            for t in range(REL_BUCKETS):
                val = jnp.where((b == t) & ((row >> 1) == h), tab_ref[t, h] * LOG2E, val)
        bd_ref[g] = val
    rows = lax.broadcasted_iota(jnp.int32, (2 * DIFF_HEADS, LANES), 0)
    b0 = _bucket(jnp.zeros((2 * DIFF_HEADS, LANES), jnp.int32))
    vs = jnp.zeros((2 * DIFF_HEADS, LANES), F32)
    for h in range(DIFF_HEADS):
        for t in range(REL_BUCKETS):
            vs = jnp.where((b0 == t) & ((rows >> 1) == h), tab_ref[t, h] * LOG2E, vs)
    bs_ref[...] = vs


def _bias_tables(table, past_len):
    return pl.pallas_call(
        functools.partial(_bias_kernel, past_len=past_len),
        in_specs=[pl.BlockSpec(memory_space=pltpu.SMEM)],
        out_shape=(jax.ShapeDtypeStruct((3 * DIFF_HEADS + 1, BK, BQ), F32),
                   jax.ShapeDtypeStruct((past_len // (DEC_PAGES * PAGE), 2 * DIFF_HEADS, DEC_PAGES * PAGE), F32),
                   jax.ShapeDtypeStruct((2 * DIFF_HEADS, LANES), F32)),
        compiler_params=pltpu.CompilerParams(vmem_limit_bytes=VMEM_LIMIT),
        name="bias_tables",
    )(table.astype(F32))


def _fold_kernel(nope_ref, uk_ref, uv_ref, bm_ref, wfold_ref, wf2_ref):
    for h in range(MLA_HEADS):
        wfold_ref[:, h * LANES:(h + 1) * LANES] = lax.dot_general(
            nope_ref[h], uk_ref[h], (((1,), (1,)), ((), ())),
            preferred_element_type=F32, precision=lax.Precision.HIGHEST)
        wf2_ref[h * LANES:(h + 1) * LANES, :] = jnp.dot(
            uv_ref[h], bm_ref[h], preferred_element_type=F32, precision=lax.Precision.HIGHEST)


def _fold_weights(w_uq, w_uk, w_uv, w_bm):
    uq = w_uq.reshape(MLA_QR, MLA_HEADS, MLA_NOPE + MLA_ROPE)
    nope = jnp.transpose(uq[:, :, :MLA_NOPE], (1, 0, 2))
    uk = jnp.transpose(w_uk, (1, 0, 2))
    uv = jnp.transpose(w_uv, (1, 0, 2))
    bm = w_bm.reshape(MLA_HEADS, MLA_V, D_MODEL)
    return pl.pallas_call(
        _fold_kernel,
        out_shape=(jax.ShapeDtypeStruct((MLA_QR, MLA_HEADS * LANES), F32),
                   jax.ShapeDtypeStruct((MLA_HEADS * LANES, D_MODEL), F32)),
        compiler_params=pltpu.CompilerParams(vmem_limit_bytes=VMEM_LIMIT),
        name="fold_weights",
    )(nope, uk, uv, bm)


def _rms(x, g, eps):
    return x * lax.rsqrt(jnp.mean(x * x, axis=-1, keepdims=True) + eps) * g


def _proj_kernel(x_ref, g_ref, win_ref, gq_ref, wq2_ref, gkv_ref, cq_ref, sq_ref, ck_ref, sk_ref,
                 ckv_ref, kpet_ref, dkt_ref, dv_ref, kvcat_ref, ckvt_ref, qcat_ref, dq_ref, dk16_ref, dvt_ref,
                 ga_ref, gb_ref):
    tm = x_ref.shape[0]
    h = _rms(x_ref[...], g_ref[...], NORM_EPS).astype(BF16)

    def seg(lo, hi):
        return jnp.dot(h, win_ref[:, lo:hi], preferred_element_type=F32)

    qn = _rms(seg(C_QLAT, C_KV), gq_ref[...], NORM_EPS).astype(BF16)
    zq = jnp.dot(qn, wq2_ref[...], preferred_element_type=F32)
    nq = MLA_HEADS * LANES
    qscale = MLA_SCALE * LOG2E
    qpe = (zq[:, nq:nq + 2 * LANES] * cq_ref[...] + zq[:, nq + 2 * LANES:] * sq_ref[...]) * qscale
    lane = lax.broadcasted_iota(jnp.int32, (tm, LANES), 1)
    per_block = LANES // MLA_ROPE
    for hd in range(MLA_HEADS):
        blk = qpe[:, (hd // per_block) * LANES:(hd // per_block + 1) * LANES]
        sh = (hd % per_block) * MLA_ROPE
        if sh:
            blk = pltpu.roll(blk, LANES - sh, axis=1)
        blk = jnp.where(lane < MLA_ROPE, blk, 0.0)
        qcat_ref[:, hd * QCAT:hd * QCAT + LANES] = (zq[:, hd * LANES:(hd + 1) * LANES] * qscale).astype(BF16)
        qcat_ref[:, hd * QCAT + LANES:(hd + 1) * QCAT] = blk.astype(BF16)

    ckv = _rms(seg(C_KV, C_KPE), gkv_ref[...], NORM_EPS)
    kpe = seg(C_KPE, C_KPESW) * ck_ref[...] + seg(C_KPESW, C_DQ) * sk_ref[...]
    ckv_ref[...] = ckv
    kpet_ref[0] = kpe.T[:MLA_ROPE]
    kvcat_ref[:, :LANES] = ckv.astype(BF16)
    kvcat_ref[:, LANES:] = kpe.astype(BF16)
    ckvt_ref[0] = ckv.T.astype(BF16)

    dq_ref[...] = (seg(C_DQ, C_DK) * LOG2E).astype(BF16)
    dk = seg(C_DK, C_DV)
    dkt_ref[0] = dk.T
    dk16_ref[...] = dk.astype(BF16)
    dv = seg(C_DV, C_GA)
    dv_ref[...] = dv
    for hd in range(DIFF_HEADS):
        dvt_ref[0, hd] = dv[:, hd * DIFF_V:(hd + 1) * DIFF_V].T.astype(BF16)

    ga_ref[...] = jax.nn.sigmoid(seg(C_GA, C_GB))
    gb_ref[...] = jax.nn.sigmoid(seg(C_GB, C_END))


def _proj(x, tables, tm, weights):
    g_attn, w_in_r, g_q, w_q2, g_kv = weights
    cq, sq, ck, sk = tables
    n = x.shape[0]
    nb = n // tm
    seq = cq.shape[0]
    tb = seq // tm
    batch = n // seq
    tcol = lambda i: (i // tb, 0, i % tb)
    row = lambda i: (i, 0)
    trow = lambda i: (i % tb, 0)
    in_specs = [
        pl.BlockSpec((tm, D_MODEL), row),
        _const_spec((1, D_MODEL)),
        _const_spec((D_MODEL, C_END)),
        _const_spec((1, MLA_QR)),
        _const_spec((MLA_QR, w_q2.shape[1])),
        _const_spec((1, MLA_KVR)),
        pl.BlockSpec((tm, 2 * LANES), trow), pl.BlockSpec((tm, 2 * LANES), trow),
        pl.BlockSpec((tm, LANES), trow), pl.BlockSpec((tm, LANES), trow),
    ]
    out_shape = (
        jax.ShapeDtypeStruct((n, MLA_KVR), F32),
        jax.ShapeDtypeStruct((batch, MLA_ROPE, seq), F32),
        jax.ShapeDtypeStruct((batch, 2 * DIFF_HEADS * DIFF_QK, seq), F32),
        jax.ShapeDtypeStruct((n, DIFF_HEADS * DIFF_V), F32),
        jax.ShapeDtypeStruct((n, QCAT), BF16),
        jax.ShapeDtypeStruct((nb, MLA_KVR, tm), BF16),
        jax.ShapeDtypeStruct((n, MLA_HEADS * QCAT), BF16),
        jax.ShapeDtypeStruct((n, 2 * DIFF_HEADS * DIFF_QK), BF16),
        jax.ShapeDtypeStruct((n, 2 * DIFF_HEADS * DIFF_QK), BF16),
        jax.ShapeDtypeStruct((nb, DIFF_HEADS, DIFF_V, tm), BF16),
        jax.ShapeDtypeStruct((n, D_MODEL), F32),
        jax.ShapeDtypeStruct((n, D_MODEL), F32),
    )
    out_specs = (
        pl.BlockSpec((tm, MLA_KVR), row),
        pl.BlockSpec((1, MLA_ROPE, tm), tcol),
        pl.BlockSpec((1, 512, tm), tcol),
        pl.BlockSpec((tm, 512), row),
        pl.BlockSpec((tm, QCAT), row),
        pl.BlockSpec((1, MLA_KVR, tm), lambda i: (i, 0, 0)),
        pl.BlockSpec((tm, MLA_HEADS * QCAT), row),
        pl.BlockSpec((tm, 512), row),
        pl.BlockSpec((tm, 512), row),
        pl.BlockSpec((1, DIFF_HEADS, DIFF_V, tm), lambda i: (i, 0, 0, 0)),
        pl.BlockSpec((tm, D_MODEL), row),
        pl.BlockSpec((tm, D_MODEL), row),
    )
    return pl.pallas_call(
        _proj_kernel,
        grid=(nb,),
        in_specs=in_specs,
        out_specs=out_specs,
        out_shape=out_shape,
        compiler_params=_cparams(("parallel",)),
        name="proj_in",
    )(x, g_attn, w_in_r, g_q, w_q2, g_kv, cq, sq, ck, sk)


def _lambda(lq1_ref, lk1_ref, lq2_ref, lk2_ref):
    a = jnp.sum(lq1_ref[...] * lk1_ref[...], axis=-1, keepdims=True)
    b = jnp.sum(lq2_ref[...] * lk2_ref[...], axis=-1, keepdims=True)
    return jnp.exp(a) - jnp.exp(b) + LAM_INIT


NQ_MLA = MLA_HEADS * BQ
NQ_DIFF = 2 * BQ
GROUP_KEYS = DEC_PAGES * PAGE
N_SLOTS = 2


def _fused_kernel(pt_ref,
                  qcat_ref, dq_ref, kv_ref, ckvt_ref, dk_ref, dvt_ref, bias_ref,
                  lq1_ref, lk1_ref, lq2_ref, lk2_ref, subg_ref, subgr_ref,
                  sq_ref, sdq_ref, skv_ref, sdk_ref, sdv_ref, bd_ref, bs_ref,
                  cckv_hbm, ckpe_hbm, cdk_hbm, cdv_hbm,
                  olat_ref, odiff_ref, solat_ref, sodiff_ref,
                  qall_scr, qd_scr, mm_scr, lm_scr, am_scr, md_scr, ld_scr, ad_scr,
                  ckv_buf, kpe_buf, dkt_buf, dv_buf, sem,
                  smm_scr, slm_scr, sam_scr, smd_scr, sld_scr, sad_scr,
                  *, n_pages, rows_per_step):
    b = pl.program_id(0)
    i = pl.program_id(1)
    nq = pl.num_programs(1)
    step = b * nq + i
    n_steps = pl.num_programs(0) * nq
    groups_per_row = n_pages // DEC_PAGES
    n_groups = rows_per_step * groups_per_row
    nrow = 2 * DIFF_HEADS

    def group_copies(row, grp, slot):
        cps = []
        for k in range(DEC_PAGES):
            pg = pt_ref[row * n_pages + grp * DEC_PAGES + k]
            keys = pl.ds(k * PAGE, PAGE)
            cps.append(pltpu.make_async_copy(cckv_hbm.at[pg], ckv_buf.at[slot, keys, :], sem.at[slot, 0]))
            cps.append(pltpu.make_async_copy(ckpe_hbm.at[pg], kpe_buf.at[slot, :, keys], sem.at[slot, 1]))
            cps.append(pltpu.make_async_copy(cdk_hbm.at[pg], dkt_buf.at[slot, :, keys], sem.at[slot, 2]))
            cps.append(pltpu.make_async_copy(cdv_hbm.at[pg],
                                             dv_buf.at[slot, pl.ds(k * PAGE * DIFF_HEADS, PAGE * DIFF_HEADS), :],
                                             sem.at[slot, 3]))
        return cps

    def start_group(row, grp, slot):
        for c in group_copies(row, grp, slot):
            c.start()

    def wait_group(row, grp, slot):
        for c in group_copies(row, grp, slot):
            c.wait()

    @pl.when(step == 0)
    def _():
        start_group(0, 0, 0)

    def decode_group(g):
        r = g // groups_per_row
        grp = g % groups_per_row
        slot = g % N_SLOTS
        row = step * rows_per_step + r
        wait_group(row, grp, slot)

        @pl.when(g + 1 < n_groups)
        def _():
            g1 = g + 1
            start_group(step * rows_per_step + g1 // groups_per_row, g1 % groups_per_row, g1 % N_SLOTS)

        @pl.when(jnp.logical_and(g + 1 == n_groups, step + 1 < n_steps))
        def _():
            start_group((step + 1) * rows_per_step, 0, n_groups % N_SLOTS)

        q = sq_ref[r]
        qa = q[:, :MLA_KVR]
        qp = q[:, MLA_KVR:MLA_KVR + MLA_ROPE]
        rowi = lax.broadcasted_iota(jnp.int32, (nrow, 512), 0)
        lanei = lax.broadcasted_iota(jnp.int32, (nrow, 512), 1)
        dq_b = jnp.broadcast_to(sdq_ref[r].astype(F32), (nrow, 512))
        qbd = jnp.where((lanei >> 6) == rowi, dq_b, 0.0).astype(BF16)

        @pl.when(grp == 0)
        def _():
            kvs = skv_ref[r].astype(F32)
            smm_scr[...] = jnp.sum(q.astype(F32) * kvs, axis=1, keepdims=True)
            slm_scr[...] = jnp.ones(slm_scr.shape, F32)
            sam_scr[...] = jnp.broadcast_to(kvs[:, :MLA_KVR], sam_scr.shape)
            s0 = jnp.sum(qbd.astype(F32) * sdk_ref[r].astype(F32), axis=1, keepdims=True)
            smd_scr[...] = s0 + bs_ref[:, 0:1]
            sld_scr[...] = jnp.ones(sld_scr.shape, F32)
            sad_scr[...] = jnp.broadcast_to(sdv_ref[r].astype(F32), sad_scr.shape)

        tdims = (((1,), (1,)), ((), ()))
        ckv16 = ckv_buf[slot].astype(BF16)
        sm = (lax.dot_general(qa, ckv16, tdims, preferred_element_type=F32)
              + jnp.dot(qp, kpe_buf[slot].astype(BF16), preferred_element_type=F32))
        sd = jnp.dot(qbd, dkt_buf[slot].astype(BF16), preferred_element_type=F32) + bd_ref[grp]

        def merge(s, m_scr, l_scr):
            m_old = m_scr[...]
            m_new = jnp.maximum(m_old, jnp.max(s, axis=1, keepdims=True))
            a = jnp.exp2(m_old - m_new)
            p = jnp.exp2(s - m_new)
            l_scr[...] = l_scr[...] * a + jnp.sum(p, axis=1, keepdims=True)
            m_scr[...] = m_new
            return a, p.astype(BF16)

        a, p = merge(sm, smm_scr, slm_scr)
        sam_scr[...] = sam_scr[...] * a + jnp.dot(p, ckv16, preferred_element_type=F32)
        a, p = merge(sd, smd_scr, sld_scr)
        for hd in range(DIFF_HEADS):
            v_h = dv_buf[slot, pl.ds(hd, GROUP_KEYS, stride=DIFF_HEADS), :].astype(BF16)
            cols = slice(hd * DIFF_V, (hd + 1) * DIFF_V)
            sad_scr[:, cols] = sad_scr[:, cols] * a + jnp.dot(p, v_h, preferred_element_type=F32)

        @pl.when(grp == groups_per_row - 1)
        def _():
            solat_ref[r] = sam_scr[...] / slm_scr[...]
            lam = _lambda(lq1_ref, lk1_ref, lq2_ref, lk2_ref)
            o = sad_scr[...] / sld_scr[...]
            for hd in range(DIFF_HEADS):
                o1 = o[2 * hd:2 * hd + 1, hd * DIFF_V:(hd + 1) * DIFF_V]
                o2 = o[2 * hd + 1:2 * hd + 2, hd * DIFF_V:(hd + 1) * DIFF_V]
                y = _rms(o1 - lam * o2, subgr_ref[...], SUBLN_EPS) * (1.0 - LAM_INIT)
                sodiff_ref[r, hd:hd + 1, :] = y

    for hd in range(MLA_HEADS):
        qall_scr[hd * BQ:(hd + 1) * BQ, :] = qcat_ref[:, hd * QCAT:(hd + 1) * QCAT]
    lane = lax.broadcasted_iota(jnp.int32, (BQ, LANES), 1)
    for hd in range(DIFF_HEADS):
        dqh = dq_ref[:, hd * LANES:(hd + 1) * LANES]
        qd_scr[hd, 0:BQ, :] = jnp.where(lane < DIFF_QK, dqh, jnp.zeros_like(dqh))
        qd_scr[hd, BQ:2 * BQ, :] = jnp.where(lane >= DIFF_QK, dqh, jnp.zeros_like(dqh))

    mm_scr[...] = jnp.full(mm_scr.shape, NEG, F32)
    lm_scr[...] = jnp.zeros(lm_scr.shape, F32)
    am_scr[...] = jnp.zeros(am_scr.shape, F32)
    md_scr[...] = jnp.full(md_scr.shape, NEG, F32)
    ld_scr[...] = jnp.zeros(ld_scr.shape, F32)
    ad_scr[...] = jnp.zeros(ad_scr.shape, F32)

    def update(m_ref, l_ref, acc_ref, k_op, q_op, vt_op, bias):
        st = lax.dot_general(k_op, q_op, (((1,), (1,)), ((), ())), preferred_element_type=F32)
        if bias is not None:
            st = st + bias
        m_old = m_ref[...]
        m_new = jnp.maximum(m_old, jnp.max(st, axis=0, keepdims=True))
        a = jnp.exp2(m_old - m_new)
        pt = jnp.exp2(st - m_new)
        l_ref[...] = l_ref[...] * a + jnp.sum(pt, axis=0, keepdims=True)
        acc_ref[...] = acc_ref[...] * a + jnp.dot(vt_op, pt.astype(BF16), preferred_element_type=F32)
        m_ref[...] = m_new

    def block(j, diag):
        koff = pl.multiple_of(j * BK, BK)
        bias = None
        if diag:
            mask = bias_ref[3 * DIFF_HEADS]
            bias = jnp.concatenate([mask] * MLA_HEADS, axis=1)
        update(mm_scr, lm_scr, am_scr, kv_ref[pl.ds(koff, BK), :], qall_scr[...], ckvt_ref[j], bias)
        d = 0 if diag else jnp.minimum(i - j, 2)
        for hd in range(DIFF_HEADS):
            kb = dk_ref[pl.ds(koff, BK), hd * LANES:(hd + 1) * LANES]
            b1 = bias_ref[d * DIFF_HEADS + hd]
            update(md_scr.at[hd], ld_scr.at[hd], ad_scr.at[hd], kb, qd_scr[hd], dvt_ref[j, hd],
                   jnp.concatenate([b1, b1], axis=1))

    def body(j, carry):
        decode_group(j)
        block(j, False)
        return carry

    lax.fori_loop(0, i, body, 0)
    decode_group(i)
    block(i, True)

    def tail(j, carry):
        decode_group(j)
        return carry

    lax.fori_loop(i + 1, n_groups, tail, 0)

    o = am_scr[...] / lm_scr[...]
    for hd in range(MLA_HEADS):
        olat_ref[:, hd * LANES:(hd + 1) * LANES] = o[:, hd * BQ:(hd + 1) * BQ].T.astype(BF16)
    lam = _lambda(lq1_ref, lk1_ref, lq2_ref, lk2_ref)
    for hd in range(DIFF_HEADS):
        op = ad_scr[hd] / ld_scr[hd]
        od = op[:, :BQ] - lam * op[:, BQ:]
        y = od * lax.rsqrt(jnp.mean(od * od, axis=0, keepdims=True) + SUBLN_EPS) * subg_ref[...]
        odiff_ref[:, hd * LANES:(hd + 1) * LANES] = (y * (1.0 - LAM_INIT)).T.astype(BF16)


def _attn_fused(p_in, s_in, caches, page_table, bias, bias_dec, bias_self, lam_vecs, subg, batch, seq):
    qcat, dq, kvcat, ckvt, dk16, dvt = p_in
    s_qcat, s_dq, s_kvcat, s_dk16, s_dv16 = s_in
    nq = seq // BQ
    nk = seq // BK
    n = batch * seq
    nb, n_pages = page_table.shape
    n_steps = batch * nq
    assert nb % n_steps == 0 and n_pages % DEC_PAGES == 0
    rps = nb // n_steps
    n_groups = rps * (n_pages // DEC_PAGES)
    assert n_groups >= nq and n_groups % N_SLOTS == 0
    pt = page_table.reshape(-1).astype(jnp.int32)

    qrow = lambda b, i, pt: (b * nq + i, 0)
    srow = lambda b, i, pt: (b * nq + i, 0, 0)

    def const(shape):
        nd = len(shape)
        return pl.BlockSpec(shape, lambda b, i, pt: (0,) * nd, pipeline_mode=pl.Buffered(1))

    in_specs = [
        pl.BlockSpec((BQ, MLA_HEADS * QCAT), qrow),
        pl.BlockSpec((BQ, 512), qrow),
        pl.BlockSpec((seq, QCAT), lambda b, i, pt: (b, 0)),
        pl.BlockSpec((nk, MLA_KVR, BK), lambda b, i, pt: (b, 0, 0)),
        pl.BlockSpec((seq, 512), lambda b, i, pt: (b, 0)),
        pl.BlockSpec((nk, DIFF_HEADS, DIFF_V, BK), lambda b, i, pt: (b, 0, 0, 0)),
        const(bias.shape),
        const((1, DIFF_QK)), const((1, DIFF_QK)), const((1, DIFF_QK)), const((1, DIFF_QK)),
        const((DIFF_V, 1)), const((1, DIFF_V)),
        pl.BlockSpec((rps, MLA_HEADS, QCAT), srow),
        pl.BlockSpec((rps, 1, 512), srow),
        pl.BlockSpec((rps, 1, QCAT), srow),
        pl.BlockSpec((rps, 1, 512), srow),
        pl.BlockSpec((rps, 1, 512), srow),
        const(bias_dec.shape),
        const(bias_self.shape),
        pl.BlockSpec(memory_space=pl.ANY), pl.BlockSpec(memory_space=pl.ANY),
        pl.BlockSpec(memory_space=pl.ANY), pl.BlockSpec(memory_space=pl.ANY),
    ]
    out_specs = (pl.BlockSpec((BQ, MLA_HEADS * LANES), qrow), pl.BlockSpec((BQ, DIFF_HEADS * DIFF_V), qrow),
                 pl.BlockSpec((rps, MLA_HEADS, LANES), srow), pl.BlockSpec((rps, DIFF_HEADS, DIFF_V), srow))
    out_shape = (jax.ShapeDtypeStruct((n, MLA_HEADS * LANES), BF16),
                 jax.ShapeDtypeStruct((n, DIFF_HEADS * DIFF_V), BF16),
                 jax.ShapeDtypeStruct((nb, MLA_HEADS, LANES), F32),
                 jax.ShapeDtypeStruct((nb, DIFF_HEADS, DIFF_V), F32))
    nrow = 2 * DIFF_HEADS
    scratch = [
        pltpu.VMEM((NQ_MLA, QCAT), BF16),
        pltpu.VMEM((DIFF_HEADS, NQ_DIFF, LANES), BF16),
        pltpu.VMEM((1, NQ_MLA), F32), pltpu.VMEM((1, NQ_MLA), F32), pltpu.VMEM((MLA_KVR, NQ_MLA), F32),
        pltpu.VMEM((DIFF_HEADS, 1, NQ_DIFF), F32), pltpu.VMEM((DIFF_HEADS, 1, NQ_DIFF), F32),
        pltpu.VMEM((DIFF_HEADS, DIFF_V, NQ_DIFF), F32),
        pltpu.VMEM((N_SLOTS, GROUP_KEYS, MLA_KVR), F32),
        pltpu.VMEM((N_SLOTS, MLA_ROPE, GROUP_KEYS), F32),
        pltpu.VMEM((N_SLOTS, 512, GROUP_KEYS), F32),
        pltpu.VMEM((N_SLOTS, GROUP_KEYS * DIFF_HEADS, DIFF_V), F32),
        pltpu.SemaphoreType.DMA((N_SLOTS, 4)),
        pltpu.VMEM((MLA_HEADS, 1), F32), pltpu.VMEM((MLA_HEADS, 1), F32), pltpu.VMEM((MLA_HEADS, MLA_KVR), F32),
        pltpu.VMEM((nrow, 1), F32), pltpu.VMEM((nrow, 1), F32), pltpu.VMEM((nrow, 512), F32),
    ]
    grid_spec = pltpu.PrefetchScalarGridSpec(
        num_scalar_prefetch=1, grid=(batch, nq), in_specs=in_specs, out_specs=out_specs, scratch_shapes=scratch)
    olat, odiff, s_olat, s_odiff = pl.pallas_call(
        functools.partial(_fused_kernel, n_pages=n_pages, rows_per_step=rps),
        grid_spec=grid_spec,
        out_shape=out_shape,
        compiler_params=_cparams(("arbitrary", "arbitrary")),
        name="attn_fused",
    )(pt, qcat, dq, kvcat, ckvt, dk16, dvt, bias, *lam_vecs, subg.reshape(-1, 1), subg.reshape(1, -1),
      s_qcat.reshape(nb, MLA_HEADS, QCAT), s_dq.reshape(nb, 1, 512), s_kvcat.reshape(nb, 1, QCAT),
      s_dk16.reshape(nb, 1, 512), s_dv16.reshape(nb, 1, 512), bias_dec, bias_self, *caches)
    return (olat, odiff, s_olat.reshape(nb, MLA_HEADS * LANES).astype(BF16),
            s_odiff.reshape(nb, DIFF_HEADS * DIFF_V).astype(BF16))


def _post_kernel(x_ref, olat_ref, odiff_ref, ga_ref, gb_ref, wf2_ref, wbd_ref, wo_ref,
                 gffn_ref, wg_ref, wu_ref, wd_ref, gfin_ref, y_ref):
    bm = jnp.dot(olat_ref[...], wf2_ref[...], preferred_element_type=F32)
    bd = jnp.dot(odiff_ref[...], wbd_ref[...], preferred_element_type=F32)
    merged = (ga_ref[...] * bm + gb_ref[...] * bd).astype(BF16)
    x = x_ref[...] + jnp.dot(merged, wo_ref[...], preferred_element_type=F32)
    h2 = _rms(x, gffn_ref[...], NORM_EPS).astype(BF16)
    gate = jnp.dot(h2, wg_ref[...], preferred_element_type=F32)
    up = jnp.dot(h2, wu_ref[...], preferred_element_type=F32)
    act = (gate * jax.nn.sigmoid(gate) * up).astype(BF16)
    x = x + jnp.dot(act, wd_ref[...], preferred_element_type=F32)
    y_ref[...] = _rms(x, gfin_ref[...], NORM_EPS)


def _post(x, olat, odiff, ga, gb, weights, tm):
    n = x.shape[0]
    row = lambda i: (i, 0)
    wf2, wbd, wo, gffn, wg, wu, wd, gfin = weights
    single = pl.Buffered(1)

    def wspec(shape):
        return pl.BlockSpec(shape, lambda i: (0, 0), pipeline_mode=single)

    in_specs = [
        pl.BlockSpec((tm, D_MODEL), row),
        pl.BlockSpec((tm, MLA_HEADS * LANES), row),
        pl.BlockSpec((tm, 512), row),
        pl.BlockSpec((tm, D_MODEL), row),
        pl.BlockSpec((tm, D_MODEL), row),
        wspec(wf2.shape), wspec(wbd.shape), wspec(wo.shape), wspec(gffn.shape),
        wspec(wg.shape), wspec(wu.shape), wspec(wd.shape), wspec(gfin.shape),
    ]
    return pl.pallas_call(
        _post_kernel,
        grid=(n // tm,),
        in_specs=in_specs,
        out_specs=pl.BlockSpec((tm, D_MODEL), row),
        out_shape=jax.ShapeDtypeStruct((n, D_MODEL), F32),
        compiler_params=_cparams(("parallel",)),
        name="merge_ffn",
    )(x, olat, odiff, ga, gb, wf2, wbd, wo, gffn, wg, wu, wd, gfin)


def kernel(x_prompt, x_sample, cache_mla_ckv, cache_mla_kpe, cache_diff_k, cache_diff_v, page_table, rel_bias_table, norm_attn_g, w_in, mla_q_norm_g, mla_w_uq, mla_kv_norm_g, mla_w_uk, mla_w_uv, diff_lambda_q1, diff_lambda_k1, diff_lambda_q2, diff_lambda_k2, diff_subln_g, w_branch_mla, w_branch_diff, w_o, norm_ffn_g, w_ffn_gate, w_ffn_up, w_ffn_down, norm_final_g):
    batch, seq, _ = x_prompt.shape
    nb, dec_seq, _ = x_sample.shape
    assert dec_seq == 1 and w_in.shape[0] == 1
    n_pages = page_table.shape[1]
    past_len = n_pages * PAGE
    assert REL_MAX_DIST <= BQ + 1 and seq % BQ == 0
    n_pool = cache_mla_ckv.shape[1]

    w = w_in[0]
    sw = np.concatenate([np.arange(16, 32), np.arange(0, 16)])
    z96 = jnp.zeros((D_MODEL, LANES - MLA_ROPE), F32)
    w_kpe = w[:, 512:544]
    w_in_r = jnp.concatenate(
        [w[:, 0:384], w[:, 384:512], w_kpe, z96, w_kpe[:, sw], z96,
         w[:, 544:1056] * DIFF_SCALE, w[:, 1056:1568], w[:, 1568:2080], w[:, 2080:3104], w[:, 3104:4128]],
        axis=1).astype(BF16)
    w_fold, w_f2 = _fold_weights(mla_w_uq[0], mla_w_uk[0], mla_w_uv[0], w_branch_mla[0])
    uq = mla_w_uq[0].reshape(MLA_QR, MLA_HEADS, MLA_NOPE + MLA_ROPE)
    w_pe = uq[:, :, MLA_NOPE:]
    w_q2 = jnp.concatenate([w_fold, w_pe.reshape(MLA_QR, -1), w_pe[:, :, sw].reshape(MLA_QR, -1)],
                           axis=1).astype(BF16)
    proj_w = (norm_attn_g[0].reshape(1, -1), w_in_r, mla_q_norm_g[0].reshape(1, -1), w_q2,
              mla_kv_norm_g[0].reshape(1, -1))
    post_w = (w_f2.astype(BF16), w_branch_diff[0].astype(BF16), w_o[0].astype(BF16),
              norm_ffn_g[0].reshape(1, -1), w_ffn_gate[0].astype(BF16), w_ffn_up[0].astype(BF16),
              w_ffn_down[0].astype(BF16), norm_final_g.reshape(1, -1))
    lam_vecs = tuple(v[0].reshape(1, -1) for v in (diff_lambda_q1, diff_lambda_k1, diff_lambda_q2, diff_lambda_k2))
    subg = diff_subln_g[0]

    tab_p = _rope_tables(jnp.arange(seq, dtype=jnp.int32))
    tab_s = _rope_tables(jnp.full((nb,), past_len, dtype=jnp.int32))
    bias_p, bias_dec, bias_self = _bias_tables(rel_bias_table, past_len)

    xp = x_prompt.reshape(batch * seq, D_MODEL)
    (p_ckv, p_kpe, p_dk, p_dv, p_kvcat, p_ckvt, p_qcat, p_dq, p_dk16, p_dvt, p_ga, p_gb) = _proj(
        xp, tab_p, BK, proj_w)
    xs = x_sample.reshape(nb, D_MODEL)
    (s_ckv, s_kpe, s_dk, s_dv, s_kvcat, _, s_qcat, s_dq, s_dk16, s_dvt, s_ga, s_gb) = _proj(
        xs, tab_s, nb, proj_w)

    caches = (cache_mla_ckv.reshape(n_pool, PAGE, MLA_KVR),
              jnp.transpose(cache_mla_kpe, (0, 1, 3, 2)).reshape(n_pool, MLA_ROPE, PAGE),
              jnp.transpose(cache_diff_k, (0, 1, 3, 4, 5, 2)).reshape(n_pool, 512, PAGE),
              cache_diff_v.reshape(n_pool, PAGE * DIFF_HEADS, DIFF_V))
    p_olat, p_odiff, s_olat, s_odiff = _attn_fused(
        (p_qcat, p_dq, p_kvcat, p_ckvt, p_dk16, p_dvt),
        (s_qcat, s_dq, s_kvcat, s_dk16, s_dv.astype(BF16)),
        caches, page_table, bias_p, bias_dec, bias_self, lam_vecs, subg, batch, seq)

    y_prompt = _post(xp, p_olat, p_odiff, p_ga, p_gb, post_w, 256).reshape(batch, seq, D_MODEL)
    y_sample = _post(xs, s_olat, s_odiff, s_ga, s_gb, post_w, nb).reshape(nb, 1, D_MODEL)

    def kpe_out(t, b, s):
        return jnp.transpose(t, (0, 2, 1)).reshape(1, b, s, MLA_ROPE)

    def dk_out(t, b, s):
        t = t.reshape(b, DIFF_HEADS, 2, DIFF_QK, s)
        return jnp.transpose(t, (0, 4, 1, 2, 3)).reshape(1, b, s, DIFF_HEADS, 2, DIFF_QK)

    return (y_prompt, y_sample,
            p_ckv.reshape(1, batch, seq, MLA_KVR), kpe_out(p_kpe, batch, seq),
            dk_out(p_dk, batch, seq), p_dv.reshape(1, batch, seq, DIFF_HEADS, DIFF_V),
            s_ckv.reshape(1, nb, 1, MLA_KVR), kpe_out(s_kpe, 1, nb).reshape(1, nb, 1, MLA_ROPE),
            dk_out(s_dk, 1, nb).reshape(1, nb, 1, DIFF_HEADS, 2, DIFF_QK),
            s_dv.reshape(1, nb, 1, DIFF_HEADS, DIFF_V))
```

```python
import functools
import math

import numpy as np
import jax
import jax.numpy as jnp
from jax import lax
from jax.experimental import pallas as pl
from jax.experimental.pallas import tpu as pltpu

F32 = jnp.float32
BF16 = jnp.bfloat16

D_MODEL = 1024
PAGE = 128
MLA_HEADS = 8
MLA_NOPE = 64
MLA_ROPE = 32
MLA_V = 64
MLA_QR = 384
MLA_KVR = 128
MLA_SCALE = (MLA_NOPE + MLA_ROPE) ** -0.5
ROPE_THETA = 10000.0
DIFF_HEADS = 4
DIFF_QK = 64
DIFF_V = 128
DIFF_SCALE = DIFF_QK ** -0.5
SUBLN_EPS = 1e-5
REL_BUCKETS = 32
REL_MAX_DIST = 128
FFN_DIM = 2816
NORM_EPS = 1e-6
LAM_INIT = 0.8 - 0.6 * math.exp(-0.3 * 0)

LANES = 128
QCAT = 2 * LANES
NEG = -1e30
LOG2E = math.log2(math.e)
VMEM_LIMIT = 56 * 1024 * 1024

C_QLAT = 0
C_KV = 384
C_KPE = 512
C_KPESW = 640
C_DQ = 768
C_DK = 1280
C_DV = 1792
C_GA = 2304
C_GB = 3328
C_END = 4352

BQ = 256
BK = 256
DEC_PAGES = 16
VT_ROWS = MLA_KVR + 16


def _cparams(sem):
    return pltpu.CompilerParams(dimension_semantics=sem, vmem_limit_bytes=VMEM_LIMIT)


def _const_spec(shape):
    nd = len(shape)
    return pl.BlockSpec(shape, lambda *_: (0,) * nd, pipeline_mode=pl.Buffered(1))


def _rope_tables_kernel(pos_ref, inv_ref, cq_ref, sq_ref, ck_ref, sk_ref):
    ang = pos_ref[...] * inv_ref[...]
    c = jnp.cos(ang)
    s = jnp.sin(ang)
    lane = lax.broadcasted_iota(jnp.int32, ang.shape, 1)
    first_half = (lane & (MLA_ROPE - 1)) < (MLA_ROPE // 2)
    s_signed = jnp.where(first_half, -s, s)
    cq_ref[...] = c
    sq_ref[...] = s_signed
    kmask = lax.broadcasted_iota(jnp.int32, ck_ref.shape, 1) < MLA_ROPE
    ck_ref[...] = jnp.where(kmask, c[:, :LANES], 0.0)
    sk_ref[...] = jnp.where(kmask, s_signed[:, :LANES], 0.0)


def _rope_tables(pos):
    n = pos.shape[0]
    half = MLA_ROPE // 2
    inv = (ROPE_THETA ** (-jnp.arange(half, dtype=F32) / half))
    inv = jnp.tile(inv, 2 * MLA_HEADS).reshape(1, 2 * LANES)
    return pl.pallas_call(
        _rope_tables_kernel,
        out_shape=(jax.ShapeDtypeStruct((n, 2 * LANES), F32), jax.ShapeDtypeStruct((n, 2 * LANES), F32),
                   jax.ShapeDtypeStruct((n, LANES), F32), jax.ShapeDtypeStruct((n, LANES), F32)),
        name="rope_tables",
    )(pos.astype(F32).reshape(n, 1), inv)


def _bucket(n):
    max_exact = REL_BUCKETS // 2
    nf = jnp.maximum(n, 1).astype(F32)
    large = max_exact + (jnp.log(nf / max_exact) / math.log(REL_MAX_DIST / max_exact)
                         * (REL_BUCKETS - max_exact)).astype(jnp.int32)
    large = jnp.minimum(large, REL_BUCKETS - 1)
    return jnp.where(n < max_exact, n, large)


def _bias_kernel(tab_ref, bp_ref, bd_ref, bs_ref, *, past_len):
    kidx = lax.broadcasted_iota(jnp.int32, (BK, BQ), 0)
    qidx = lax.broadcasted_iota(jnp.int32, (BK, BQ), 1)
    for d in range(3):
        rel = d * BQ + qidx - kidx
        b = _bucket(jnp.maximum(rel, 0))
        for h in range(DIFF_HEADS):
            val = jnp.zeros((BK, BQ), F32)
            for t in range(REL_BUCKETS):
                val = jnp.where(b == t, tab_ref[t, h] * LOG2E, val)
            if d == 0:
                val = jnp.where(rel >= 0, val, NEG)
            bp_ref[d * DIFF_HEADS + h] = val
    bp_ref[3 * DIFF_HEADS] = jnp.where(qidx - kidx >= 0, 0.0, NEG)
    gk = DEC_PAGES * PAGE
    row = lax.broadcasted_iota(jnp.int32, (2 * DIFF_HEADS, gk), 0)
    for g in range(past_len // gk):
        kpos = g * gk + lax.broadcasted_iota(jnp.int32, (2 * DIFF_HEADS, gk), 1)
        b = _bucket(jnp.maximum(past_len - kpos, 0))
        val = jnp.zeros((2 * DIFF_HEADS, gk), F32)
        for h in range(DIFF_HEADS):
            for t in range(REL_BUCKETS):
                val = jnp.where((b == t) & ((row >> 1) == h), tab_ref[t, h] * LOG2E, val)
        bd_ref[g] = val
    rows = lax.broadcasted_iota(jnp.int32, (2 * DIFF_HEADS, LANES), 0)
    b0 = _bucket(jnp.zeros((2 * DIFF_HEADS, LANES), jnp.int32))
    vs = jnp.zeros((2 * DIFF_HEADS, LANES), F32)
    for h in range(DIFF_HEADS):
        for t in range(REL_BUCKETS):
            vs = jnp.where((b0 == t) & ((rows >> 1) == h), tab_ref[t, h] * LOG2E, vs)
    bs_ref[...] = vs


def _bias_tables(table, past_len):
    return pl.pallas_call(
        functools.partial(_bias_kernel, past_len=past_len),
        in_specs=[pl.BlockSpec(memory_space=pltpu.SMEM)],
        out_shape=(jax.ShapeDtypeStruct((3 * DIFF_HEADS + 1, BK, BQ), F32),
                   jax.ShapeDtypeStruct((past_len // (DEC_PAGES * PAGE), 2 * DIFF_HEADS, DEC_PAGES * PAGE), F32),
                   jax.ShapeDtypeStruct((2 * DIFF_HEADS, LANES), F32)),
        compiler_params=pltpu.CompilerParams(vmem_limit_bytes=VMEM_LIMIT),
        name="bias_tables",
    )(table.astype(F32))


def _fold_kernel(nope_ref, uk_ref, uv_ref, bm_ref, wfold_ref, wf2_ref):
    for h in range(MLA_HEADS):
        wfold_ref[:, h * LANES:(h + 1) * LANES] = lax.dot_general(
            nope_ref[h], uk_ref[h], (((1,), (1,)), ((), ())),
            preferred_element_type=F32, precision=lax.Precision.HIGHEST)
        wf2_ref[h * LANES:(h + 1) * LANES, :] = jnp.dot(
            uv_ref[h], bm_ref[h], preferred_element_type=F32, precision=lax.Precision.HIGHEST)


def _fold_weights(w_uq, w_uk, w_uv, w_bm):
    uq = w_uq.reshape(MLA_QR, MLA_HEADS, MLA_NOPE + MLA_ROPE)
    nope = jnp.transpose(uq[:, :, :MLA_NOPE], (1, 0, 2))
    uk = jnp.transpose(w_uk, (1, 0, 2))
    uv = jnp.transpose(w_uv, (1, 0, 2))
    bm = w_bm.reshape(MLA_HEADS, MLA_V, D_MODEL)
    return pl.pallas_call(
        _fold_kernel,
        out_shape=(jax.ShapeDtypeStruct((MLA_QR, MLA_HEADS * LANES), F32),
                   jax.ShapeDtypeStruct((MLA_HEADS * LANES, D_MODEL), F32)),
        compiler_params=pltpu.CompilerParams(vmem_limit_bytes=VMEM_LIMIT),
        name="fold_weights",
    )(nope, uk, uv, bm)


def _rms(x, g, eps):
    return x * lax.rsqrt(jnp.mean(x * x, axis=-1, keepdims=True) + eps) * g


def _proj_kernel(x_ref, g_ref, win_ref, gq_ref, wq2_ref, gkv_ref, cq_ref, sq_ref, ck_ref, sk_ref,
                 ckv_ref, kpet_ref, dkt_ref, dv_ref, kvcat_ref, ckvt_ref, qcat_ref, dq_ref, dk16_ref, dvt_ref,
                 ga_ref, gb_ref):
    tm = x_ref.shape[0]
    h = _rms(x_ref[...], g_ref[...], NORM_EPS).astype(BF16)

    def seg(lo, hi):
        return jnp.dot(h, win_ref[:, lo:hi], preferred_element_type=F32)

    qn = _rms(seg(C_QLAT, C_KV), gq_ref[...], NORM_EPS).astype(BF16)
    zq = jnp.dot(qn, wq2_ref[...], preferred_element_type=F32)
    nq = MLA_HEADS * LANES
    qscale = MLA_SCALE * LOG2E
    qpe = (zq[:, nq:nq + 2 * LANES] * cq_ref[...] + zq[:, nq + 2 * LANES:] * sq_ref[...]) * qscale
    lane = lax.broadcasted_iota(jnp.int32, (tm, LANES), 1)
    per_block = LANES // MLA_ROPE
    for hd in range(MLA_HEADS):
        blk = qpe[:, (hd // per_block) * LANES:(hd // per_block + 1) * LANES]
        sh = (hd % per_block) * MLA_ROPE
        if sh:
            blk = pltpu.roll(blk, LANES - sh, axis=1)
        blk = jnp.where(lane < MLA_ROPE, blk, 0.0)
        qcat_ref[:, hd * QCAT:hd * QCAT + LANES] = (zq[:, hd * LANES:(hd + 1) * LANES] * qscale).astype(BF16)
        qcat_ref[:, hd * QCAT + LANES:(hd + 1) * QCAT] = blk.astype(BF16)

    ckv = _rms(seg(C_KV, C_KPE), gkv_ref[...], NORM_EPS)
    kpe = seg(C_KPE, C_KPESW) * ck_ref[...] + seg(C_KPESW, C_DQ) * sk_ref[...]
    ckv_ref[...] = ckv
    kpet_ref[0] = kpe.T[:MLA_ROPE]
    kvcat_ref[:, :LANES] = ckv.astype(BF16)
    kvcat_ref[:, LANES:] = kpe.astype(BF16)
    ones_rows = jnp.ones((VT_ROWS - MLA_KVR, tm), BF16)
    ckvt_ref[0, 0:MLA_KVR, :] = ckv.T.astype(BF16)
    ckvt_ref[0, MLA_KVR:VT_ROWS, :] = ones_rows

    dq_ref[...] = (seg(C_DQ, C_DK) * LOG2E).astype(BF16)
    dk = seg(C_DK, C_DV)
    dkt_ref[0] = dk.T
    dk16_ref[...] = dk.astype(BF16)
    dv = seg(C_DV, C_GA)
    for hd in range(DIFF_HEADS):
        dv_ref[pl.ds(hd, tm, stride=DIFF_HEADS), :] = dv[:, hd * DIFF_V:(hd + 1) * DIFF_V]
    for hd in range(DIFF_HEADS):
        dvt_ref[0, hd, 0:DIFF_V, :] = dv[:, hd * DIFF_V:(hd + 1) * DIFF_V].T.astype(BF16)
        dvt_ref[0, hd, DIFF_V:VT_ROWS, :] = ones_rows

    ga_ref[...] = jax.nn.sigmoid(seg(C_GA, C_GB))
    gb_ref[...] = jax.nn.sigmoid(seg(C_GB, C_END))


def _proj(x, tables, tm, weights):
    g_attn, w_in_r, g_q, w_q2, g_kv = weights
    cq, sq, ck, sk = tables
    n = x.shape[0]
    nb = n // tm
    seq = cq.shape[0]
    tb = seq // tm
    batch = n // seq
    tcol = lambda i: (i // tb, 0, i % tb)
    row = lambda i: (i, 0)
    trow = lambda i: (i % tb, 0)
    in_specs = [
        pl.BlockSpec((tm, D_MODEL), row),
        _const_spec((1, D_MODEL)),
        _const_spec((D_MODEL, C_END)),
        _const_spec((1, MLA_QR)),
        _const_spec((MLA_QR, w_q2.shape[1])),
        _const_spec((1, MLA_KVR)),
        pl.BlockSpec((tm, 2 * LANES), trow), pl.BlockSpec((tm, 2 * LANES), trow),
        pl.BlockSpec((tm, LANES), trow), pl.BlockSpec((tm, LANES), trow),
    ]
    out_shape = (
        jax.ShapeDtypeStruct((n, MLA_KVR), F32),
        jax.ShapeDtypeStruct((batch, MLA_ROPE, seq), F32),
        jax.ShapeDtypeStruct((batch, 2 * DIFF_HEADS * DIFF_QK, seq), F32),
        jax.ShapeDtypeStruct((n * DIFF_HEADS, DIFF_V), F32),
        jax.ShapeDtypeStruct((n, QCAT), BF16),
        jax.ShapeDtypeStruct((nb, VT_ROWS, tm), BF16),
        jax.ShapeDtypeStruct((n, MLA_HEADS * QCAT), BF16),
        jax.ShapeDtypeStruct((n, 2 * DIFF_HEADS * DIFF_QK), BF16),
        jax.ShapeDtypeStruct((n, 2 * DIFF_HEADS * DIFF_QK), BF16),
        jax.ShapeDtypeStruct((nb, DIFF_HEADS, VT_ROWS, tm), BF16),
        jax.ShapeDtypeStruct((n, D_MODEL), F32),
        jax.ShapeDtypeStruct((n, D_MODEL), F32),
    )
    out_specs = (
        pl.BlockSpec((tm, MLA_KVR), row),
        pl.BlockSpec((1, MLA_ROPE, tm), tcol),
        pl.BlockSpec((1, 512, tm), tcol),
        pl.BlockSpec((tm * DIFF_HEADS, DIFF_V), row),
        pl.BlockSpec((tm, QCAT), row),
        pl.BlockSpec((1, VT_ROWS, tm), lambda i: (i, 0, 0)),
        pl.BlockSpec((tm, MLA_HEADS * QCAT), row),
        pl.BlockSpec((tm, 512), row),
        pl.BlockSpec((tm, 512), row),
        pl.BlockSpec((1, DIFF_HEADS, VT_ROWS, tm), lambda i: (i, 0, 0, 0)),
        pl.BlockSpec((tm, D_MODEL), row),
        pl.BlockSpec((tm, D_MODEL), row),
    )
    return pl.pallas_call(
        _proj_kernel,
        grid=(nb,),
        in_specs=in_specs,
        out_specs=out_specs,
        out_shape=out_shape,
        compiler_params=_cparams(("parallel",)),
        name="proj_in",
    )(x, g_attn, w_in_r, g_q, w_q2, g_kv, cq, sq, ck, sk)


def _lambda(lq1_ref, lk1_ref, lq2_ref, lk2_ref):
    a = jnp.sum(lq1_ref[...] * lk1_ref[...], axis=-1, keepdims=True)
    b = jnp.sum(lq2_ref[...] * lk2_ref[...], axis=-1, keepdims=True)
    return jnp.exp(a) - jnp.exp(b) + LAM_INIT


NQ_MLA = MLA_HEADS * BQ
NQ_DIFF = 2 * BQ
NQ_ALL = NQ_MLA + DIFF_HEADS * NQ_DIFF
GROUP_KEYS = DEC_PAGES * PAGE
N_SLOTS = 2


def _fused_kernel(pt_ref,
                  qcat_ref, dq_ref, kv_ref, ckvt_ref, dk_ref, dvt_ref, bias_ref,
                  lq1_ref, lk1_ref, lq2_ref, lk2_ref, subg_ref, subgr_ref,
                  sq_ref, sdq_ref, skv_ref, sdk_ref, sdv_ref, bd_ref, bs_ref,
                  cckv_hbm, ckpe_hbm, cdk_hbm, cdv_hbm,
                  olat_ref, odiff_ref, solat_ref, sodiff_ref,
                  qall_scr, qd_scr, m_scr, acc_scr,
                  ckv_buf, kpe_buf, dkt_buf, dv_buf, sem,
                  smm_scr, slm_scr, sam_scr, smd_scr, sld_scr, sad_scr,
                  *, n_pages, rows_per_step):
    b = pl.program_id(0)
    i = pl.program_id(1)
    nq = pl.num_programs(1)
    step = b * nq + i
    n_steps = pl.num_programs(0) * nq
    groups_per_row = n_pages // DEC_PAGES
    n_groups = rows_per_step * groups_per_row
    nrow = 2 * DIFF_HEADS

    def group_copies(row, grp, slot):
        cps = []
        for k in range(DEC_PAGES):
            pg = pt_ref[row * n_pages + grp * DEC_PAGES + k]
            keys = pl.ds(k * PAGE, PAGE)
            cps.append(pltpu.make_async_copy(cckv_hbm.at[pg], ckv_buf.at[slot, keys, :], sem.at[slot, 0]))
            cps.append(pltpu.make_async_copy(ckpe_hbm.at[pg], kpe_buf.at[slot, :, keys], sem.at[slot, 1]))
            cps.append(pltpu.make_async_copy(cdk_hbm.at[pg], dkt_buf.at[slot, :, keys], sem.at[slot, 2]))
            cps.append(pltpu.make_async_copy(cdv_hbm.at[pg],
                                             dv_buf.at[slot, pl.ds(k * PAGE * DIFF_HEADS, PAGE * DIFF_HEADS), :],
                                             sem.at[slot, 3]))
        return cps

    def start_group(row, grp, slot):
        for c in group_copies(row, grp, slot):
            c.start()

    def wait_group(row, grp, slot):
        for c in group_copies(row, grp, slot):
            c.wait()

    @pl.when(step == 0)
    def _():
        start_group(0, 0, 0)

    def decode_group(g):
        r = g // groups_per_row
        grp = g % groups_per_row
        slot = g % N_SLOTS
        row = step * rows_per_step + r
        wait_group(row, grp, slot)

        @pl.when(g + 1 < n_groups)
        def _():
            g1 = g + 1
            start_group(step * rows_per_step + g1 // groups_per_row, g1 % groups_per_row, g1 % N_SLOTS)

        @pl.when(jnp.logical_and(g + 1 == n_groups, step + 1 < n_steps))
        def _():
            start_group((step + 1) * rows_per_step, 0, n_groups % N_SLOTS)

        q = sq_ref[r]
        qa = q[:, :MLA_KVR]
        qp = q[:, MLA_KVR:MLA_KVR + MLA_ROPE]
        rowi = lax.broadcasted_iota(jnp.int32, (nrow, 512), 0)
        lanei = lax.broadcasted_iota(jnp.int32, (nrow, 512), 1)
        dq_b = jnp.broadcast_to(sdq_ref[r].astype(F32), (nrow, 512))
        qbd = jnp.where((lanei >> 6) == rowi, dq_b, 0.0).astype(BF16)

        @pl.when(grp == 0)
        def _():
            kvs = skv_ref[r].astype(F32)
            smm_scr[...] = jnp.sum(q.astype(F32) * kvs, axis=1, keepdims=True)
            slm_scr[...] = jnp.ones(slm_scr.shape, F32)
            sam_scr[...] = jnp.broadcast_to(kvs[:, :MLA_KVR], sam_scr.shape)
            s0 = jnp.sum(qbd.astype(F32) * sdk_ref[r].astype(F32), axis=1, keepdims=True)
            smd_scr[...] = s0 + bs_ref[:, 0:1]
            sld_scr[...] = jnp.ones(sld_scr.shape, F32)
            sad_scr[...] = jnp.broadcast_to(sdv_ref[r].astype(F32), sad_scr.shape)

        tdims = (((1,), (1,)), ((), ()))
        ckv16 = ckv_buf[slot].astype(BF16)
        sm = (lax.dot_general(qa, ckv16, tdims, preferred_element_type=F32)
              + jnp.dot(qp, kpe_buf[slot].astype(BF16), preferred_element_type=F32))
        sd = jnp.dot(qbd, dkt_buf[slot].astype(BF16), preferred_element_type=F32) + bd_ref[grp]

        def merge(s, m_scr, l_scr):
            m_old = m_scr[...]
            m_new = jnp.maximum(m_old, jnp.max(s, axis=1, keepdims=True))
            a = jnp.exp2(m_old - m_new)
            p = jnp.exp2(s - m_new)
            l_scr[...] = l_scr[...] * a + jnp.sum(p, axis=1, keepdims=True)
            m_scr[...] = m_new
            return a, p.astype(BF16)

        a, p = merge(sm, smm_scr, slm_scr)
        sam_scr[...] = sam_scr[...] * a + jnp.dot(p, ckv16, preferred_element_type=F32)
        a, p = merge(sd, smd_scr, sld_scr)
        v_all = jnp.concatenate(
            [dv_buf[slot, pl.ds(hd, GROUP_KEYS, stride=DIFF_HEADS), :].astype(BF16) for hd in range(DIFF_HEADS)],
            axis=1)
        sad_scr[...] = sad_scr[...] * a + jnp.dot(p, v_all, preferred_element_type=F32)

        @pl.when(grp == groups_per_row - 1)
        def _():
            solat_ref[r] = sam_scr[...] / slm_scr[...]
            lam = _lambda(lq1_ref, lk1_ref, lq2_ref, lk2_ref)
            o = sad_scr[...] / sld_scr[...]
            for hd in range(DIFF_HEADS):
                o1 = o[2 * hd:2 * hd + 1, hd * DIFF_V:(hd + 1) * DIFF_V]
                o2 = o[2 * hd + 1:2 * hd + 2, hd * DIFF_V:(hd + 1) * DIFF_V]
                y = _rms(o1 - lam * o2, subgr_ref[...], SUBLN_EPS) * (1.0 - LAM_INIT)
                sodiff_ref[r, hd:hd + 1, :] = y

    for hd in range(MLA_HEADS):
        qall_scr[hd * BQ:(hd + 1) * BQ, :] = qcat_ref[:, hd * QCAT:(hd + 1) * QCAT]
    lane = lax.broadcasted_iota(jnp.int32, (BQ, LANES), 1)
    for hd in range(DIFF_HEADS):
        dqh = dq_ref[:, hd * LANES:(hd + 1) * LANES]
        qd_scr[hd, 0:BQ, :] = jnp.where(lane < DIFF_QK, dqh, jnp.zeros_like(dqh))
        qd_scr[hd, BQ:2 * BQ, :] = jnp.where(lane >= DIFF_QK, dqh, jnp.zeros_like(dqh))

    m_scr[...] = jnp.full(m_scr.shape, NEG, F32)
    acc_scr[...] = jnp.zeros(acc_scr.shape, F32)
    tdims = (((1,), (1,)), ((), ()))

    def block(j, diag):
        koff = pl.multiple_of(j * BK, BK)
        st = lax.dot_general(kv_ref[pl.ds(koff, BK), :], qall_scr[...], tdims, preferred_element_type=F32)
        if diag:
            mask = bias_ref[3 * DIFF_HEADS]
            st = st + jnp.concatenate([mask] * MLA_HEADS, axis=1)
        d = 0 if diag else jnp.minimum(i - j, 2)
        parts = [st]
        for hd in range(DIFF_HEADS):
            kb = dk_ref[pl.ds(koff, BK), hd * LANES:(hd + 1) * LANES]
            b1 = bias_ref[d * DIFF_HEADS + hd]
            sh = lax.dot_general(kb, qd_scr[hd], tdims, preferred_element_type=F32)
            parts.append(sh + jnp.concatenate([b1, b1], axis=1))
        st = jnp.concatenate(parts, axis=1)
        m_old = m_scr[...]
        m_new = jnp.maximum(m_old, jnp.max(st, axis=0, keepdims=True))
        a = jnp.exp2(m_old - m_new)
        pt = jnp.exp2(st - m_new).astype(BF16)
        m_scr[...] = m_new
        cs = slice(0, NQ_MLA)
        acc_scr[:, cs] = acc_scr[:, cs] * a[:, cs] + jnp.dot(ckvt_ref[j], pt[:, cs], preferred_element_type=F32)
        for hd in range(DIFF_HEADS):
            cs = slice(NQ_MLA + hd * NQ_DIFF, NQ_MLA + (hd + 1) * NQ_DIFF)
            acc_scr[:, cs] = acc_scr[:, cs] * a[:, cs] + jnp.dot(dvt_ref[j, hd], pt[:, cs],
                                                               preferred_element_type=F32)

    def paired(j, carry):
        decode_group(j)
        block(j, False)
        return carry

    def alone(j, carry):
        block(j, False)
        return carry

    def tail(j, carry):
        decode_group(j)
        return carry

    n_paired = jnp.minimum(i, n_groups)
    lax.fori_loop(0, n_paired, paired, 0)
    lax.fori_loop(n_paired, i, alone, 0)

    @pl.when(i < n_groups)
    def _():
        decode_group(i)

    block(i, True)
    lax.fori_loop(i + 1, n_groups, tail, 0)

    acc = acc_scr[...]
    o = acc[:MLA_KVR] / acc[MLA_KVR:MLA_KVR + 1]
    for hd in range(MLA_HEADS):
        olat_ref[:, hd * LANES:(hd + 1) * LANES] = o[:, hd * BQ:(hd + 1) * BQ].T.astype(BF16)
    lam = _lambda(lq1_ref, lk1_ref, lq2_ref, lk2_ref)
    for hd in range(DIFF_HEADS):
        c0 = NQ_MLA + hd * NQ_DIFF
        od = o[:, c0:c0 + BQ] - lam * o[:, c0 + BQ:c0 + 2 * BQ]
        y = od * lax.rsqrt(jnp.mean(od * od, axis=0, keepdims=True) + SUBLN_EPS) * subg_ref[...]
        odiff_ref[:, hd * LANES:(hd + 1) * LANES] = (y * (1.0 - LAM_INIT)).T.astype(BF16)


def _attn_fused(p_in, s_in, caches, page_table, bias, bias_dec, bias_self, lam_vecs, subg, batch, seq):
    qcat, dq, kvcat, ckvt, dk16, dvt = p_in
    s_qcat, s_dq, s_kvcat, s_dk16, s_dv16 = s_in
    nq = seq // BQ
    nk = seq // BK
    n = batch * seq
    nb, n_pages = page_table.shape
    n_steps = batch * nq
    assert nb % n_steps == 0 and n_pages % DEC_PAGES == 0
    rps = nb // n_steps
    n_groups = rps * (n_pages // DEC_PAGES)
    assert n_groups % N_SLOTS == 0
    pt = page_table.reshape(-1).astype(jnp.int32)

    qrow = lambda b, i, pt: (b * nq + i, 0)
    srow = lambda b, i, pt: (b * nq + i, 0, 0)

    def const(shape):
        nd = len(shape)
        return pl.BlockSpec(shape, lambda b, i, pt: (0,) * nd, pipeline_mode=pl.Buffered(1))

    in_specs = [
        pl.BlockSpec((BQ, MLA_HEADS * QCAT), qrow),
        pl.BlockSpec((BQ, 512), qrow),
        pl.BlockSpec((seq, QCAT), lambda b, i, pt: (b, 0), pipeline_mode=pl.Buffered(1)),
        pl.BlockSpec((nk, VT_ROWS, BK), lambda b, i, pt: (b, 0, 0), pipeline_mode=pl.Buffered(1)),
        pl.BlockSpec((seq, 512), lambda b, i, pt: (b, 0), pipeline_mode=pl.Buffered(1)),
        pl.BlockSpec((nk, DIFF_HEADS, VT_ROWS, BK), lambda b, i, pt: (b, 0, 0, 0), pipeline_mode=pl.Buffered(1)),
        const(bias.shape),
        const((1, DIFF_QK)), const((1, DIFF_QK)), const((1, DIFF_QK)), const((1, DIFF_QK)),
        const((DIFF_V, 1)), const((1, DIFF_V)),
        pl.BlockSpec((rps, MLA_HEADS, QCAT), srow),
        pl.BlockSpec((rps, 1, 512), srow),
        pl.BlockSpec((rps, 1, QCAT), srow),
        pl.BlockSpec((rps, 1, 512), srow),
        pl.BlockSpec((rps, 1, 512), srow),
        const(bias_dec.shape),
        const(bias_self.shape),
        pl.BlockSpec(memory_space=pl.ANY), pl.BlockSpec(memory_space=pl.ANY),
        pl.BlockSpec(memory_space=pl.ANY), pl.BlockSpec(memory_space=pl.ANY),
    ]
    out_specs = (pl.BlockSpec((BQ, MLA_HEADS * LANES), qrow), pl.BlockSpec((BQ, DIFF_HEADS * DIFF_V), qrow),
                 pl.BlockSpec((rps, MLA_HEADS, LANES), srow), pl.BlockSpec((rps, DIFF_HEADS, DIFF_V), srow))
    out_shape = (jax.ShapeDtypeStruct((n, MLA_HEADS * LANES), BF16),
                 jax.ShapeDtypeStruct((n, DIFF_HEADS * DIFF_V), BF16),
                 jax.ShapeDtypeStruct((nb, MLA_HEADS, LANES), F32),
                 jax.ShapeDtypeStruct((nb, DIFF_HEADS, DIFF_V), F32))
    nrow = 2 * DIFF_HEADS
    scratch = [
        pltpu.VMEM((NQ_MLA, QCAT), BF16),
        pltpu.VMEM((DIFF_HEADS, NQ_DIFF, LANES), BF16),
        pltpu.VMEM((1, NQ_ALL), F32), pltpu.VMEM((VT_ROWS, NQ_ALL), F32),
        pltpu.VMEM((N_SLOTS, GROUP_KEYS, MLA_KVR), F32),
        pltpu.VMEM((N_SLOTS, MLA_ROPE, GROUP_KEYS), F32),
        pltpu.VMEM((N_SLOTS, 512, GROUP_KEYS), F32),
        pltpu.VMEM((N_SLOTS, GROUP_KEYS * DIFF_HEADS, DIFF_V), F32),
        pltpu.SemaphoreType.DMA((N_SLOTS, 4)),
        pltpu.VMEM((MLA_HEADS, 1), F32), pltpu.VMEM((MLA_HEADS, 1), F32), pltpu.VMEM((MLA_HEADS, MLA_KVR), F32),
        pltpu.VMEM((nrow, 1), F32), pltpu.VMEM((nrow, 1), F32), pltpu.VMEM((nrow, 512), F32),
    ]
    grid_spec = pltpu.PrefetchScalarGridSpec(
        num_scalar_prefetch=1, grid=(batch, nq), in_specs=in_specs, out_specs=out_specs, scratch_shapes=scratch)
    olat, odiff, s_olat, s_odiff = pl.pallas_call(
        functools.partial(_fused_kernel, n_pages=n_pages, rows_per_step=rps),
        grid_spec=grid_spec,
        out_shape=out_shape,
        compiler_params=_cparams(("arbitrary", "arbitrary")),
        name="attn_fused",
    )(pt, qcat, dq, kvcat, ckvt, dk16, dvt, bias, *lam_vecs, subg.reshape(-1, 1), subg.reshape(1, -1),
      s_qcat.reshape(nb, MLA_HEADS, QCAT), s_dq.reshape(nb, 1, 512), s_kvcat.reshape(nb, 1, QCAT),
      s_dk16.reshape(nb, 1, 512), s_dv16.reshape(nb, 1, 512), bias_dec, bias_self, *caches)
    return (olat, odiff, s_olat.reshape(nb, MLA_HEADS * LANES).astype(BF16),
            s_odiff.reshape(nb, DIFF_HEADS * DIFF_V).astype(BF16))


def _post_kernel(x_ref, olat_ref, odiff_ref, ga_ref, gb_ref, wf2_ref, wbd_ref, wo_ref,
                 gffn_ref, wg_ref, wu_ref, wd_ref, gfin_ref, y_ref):
    bm = jnp.dot(olat_ref[...], wf2_ref[...], preferred_element_type=F32)
    bd = jnp.dot(odiff_ref[...], wbd_ref[...], preferred_element_type=F32)
    merged = (ga_ref[...] * bm + gb_ref[...] * bd).astype(BF16)
    x = x_ref[...] + jnp.dot(merged, wo_ref[...], preferred_element_type=F32)
    h2 = _rms(x, gffn_ref[...], NORM_EPS).astype(BF16)
    gate = jnp.dot(h2, wg_ref[...], preferred_element_type=F32)
    up = jnp.dot(h2, wu_ref[...], preferred_element_type=F32)
    act = (gate * jax.nn.sigmoid(gate) * up).astype(BF16)
    x = x + jnp.dot(act, wd_ref[...], preferred_element_type=F32)
    y_ref[...] = _rms(x, gfin_ref[...], NORM_EPS)


def _post(x, olat, odiff, ga, gb, weights, tm):
    n = x.shape[0]
    row = lambda i: (i, 0)
    wf2, wbd, wo, gffn, wg, wu, wd, gfin = weights
    single = pl.Buffered(1)

    def wspec(shape):
        return pl.BlockSpec(shape, lambda i: (0, 0), pipeline_mode=single)

    in_specs = [
        pl.BlockSpec((tm, D_MODEL), row),
        pl.BlockSpec((tm, MLA_HEADS * LANES), row),
        pl.BlockSpec((tm, 512), row),
        pl.BlockSpec((tm, D_MODEL), row),
        pl.BlockSpec((tm, D_MODEL), row),
        wspec(wf2.shape), wspec(wbd.shape), wspec(wo.shape), wspec(gffn.shape),
        wspec(wg.shape), wspec(wu.shape), wspec(wd.shape), wspec(gfin.shape),
    ]
    return pl.pallas_call(
        _post_kernel,
        grid=(n // tm,),
        in_specs=in_specs,
        out_specs=pl.BlockSpec((tm, D_MODEL), row),
        out_shape=jax.ShapeDtypeStruct((n, D_MODEL), F32),
        compiler_params=_cparams(("parallel",)),
        name="merge_ffn",
    )(x, olat, odiff, ga, gb, wf2, wbd, wo, gffn, wg, wu, wd, gfin)


def kernel(x_prompt, x_sample, cache_mla_ckv, cache_mla_kpe, cache_diff_k, cache_diff_v, page_table, rel_bias_table, norm_attn_g, w_in, mla_q_norm_g, mla_w_uq, mla_kv_norm_g, mla_w_uk, mla_w_uv, diff_lambda_q1, diff_lambda_k1, diff_lambda_q2, diff_lambda_k2, diff_subln_g, w_branch_mla, w_branch_diff, w_o, norm_ffn_g, w_ffn_gate, w_ffn_up, w_ffn_down, norm_final_g):
    batch, seq, _ = x_prompt.shape
    nb, dec_seq, _ = x_sample.shape
    assert dec_seq == 1 and w_in.shape[0] == 1
    n_pages = page_table.shape[1]
    past_len = n_pages * PAGE
    assert REL_MAX_DIST <= BQ + 1 and seq % BQ == 0
    n_pool = cache_mla_ckv.shape[1]

    w = w_in[0]
    sw = np.concatenate([np.arange(16, 32), np.arange(0, 16)])
    z96 = jnp.zeros((D_MODEL, LANES - MLA_ROPE), F32)
    w_kpe = w[:, 512:544]
    w_in_r = jnp.concatenate(
        [w[:, 0:384], w[:, 384:512], w_kpe, z96, w_kpe[:, sw], z96,
         w[:, 544:1056] * DIFF_SCALE, w[:, 1056:1568], w[:, 1568:2080], w[:, 2080:3104], w[:, 3104:4128]],
        axis=1).astype(BF16)
    w_fold, w_f2 = _fold_weights(mla_w_uq[0], mla_w_uk[0], mla_w_uv[0], w_branch_mla[0])
    uq = mla_w_uq[0].reshape(MLA_QR, MLA_HEADS, MLA_NOPE + MLA_ROPE)
    w_pe = uq[:, :, MLA_NOPE:]
    w_q2 = jnp.concatenate([w_fold, w_pe.reshape(MLA_QR, -1), w_pe[:, :, sw].reshape(MLA_QR, -1)],
                           axis=1).astype(BF16)
    proj_w = (norm_attn_g[0].reshape(1, -1), w_in_r, mla_q_norm_g[0].reshape(1, -1), w_q2,
              mla_kv_norm_g[0].reshape(1, -1))
    post_w = (w_f2.astype(BF16), w_branch_diff[0].astype(BF16), w_o[0].astype(BF16),
              norm_ffn_g[0].reshape(1, -1), w_ffn_gate[0].astype(BF16), w_ffn_up[0].astype(BF16),
              w_ffn_down[0].astype(BF16), norm_final_g.reshape(1, -1))
    lam_vecs = tuple(v[0].reshape(1, -1) for v in (diff_lambda_q1, diff_lambda_k1, diff_lambda_q2, diff_lambda_k2))
    subg = diff_subln_g[0]

    tab_p = _rope_tables(jnp.arange(seq, dtype=jnp.int32))
    tab_s = _rope_tables(jnp.full((nb,), past_len, dtype=jnp.int32))
    bias_p, bias_dec, bias_self = _bias_tables(rel_bias_table, past_len)

    xp = x_prompt.reshape(batch * seq, D_MODEL)
    (p_ckv, p_kpe, p_dk, p_dv, p_kvcat, p_ckvt, p_qcat, p_dq, p_dk16, p_dvt, p_ga, p_gb) = _proj(
        xp, tab_p, BK, proj_w)
    xs = x_sample.reshape(nb, D_MODEL)
    (s_ckv, s_kpe, s_dk, s_dv, s_kvcat, _, s_qcat, s_dq, s_dk16, s_dvt, s_ga, s_gb) = _proj(
        xs, tab_s, nb, proj_w)

    caches = (cache_mla_ckv.reshape(n_pool, PAGE, MLA_KVR),
              jnp.transpose(cache_mla_kpe, (0, 1, 3, 2)).reshape(n_pool, MLA_ROPE, PAGE),
              jnp.transpose(cache_diff_k, (0, 1, 3, 4, 5, 2)).reshape(n_pool, 512, PAGE),
              cache_diff_v.reshape(n_pool, PAGE * DIFF_HEADS, DIFF_V))
    p_olat, p_odiff, s_olat, s_odiff = _attn_fused(
        (p_qcat, p_dq, p_kvcat, p_ckvt, p_dk16, p_dvt),
        (s_qcat, s_dq, s_kvcat, s_dk16, s_dv.reshape(nb, DIFF_HEADS * DIFF_V).astype(BF16)),
        caches, page_table, bias_p, bias_dec, bias_self, lam_vecs, subg, batch, seq)

    y_prompt = _post(xp, p_olat, p_odiff, p_ga, p_gb, post_w, 256).reshape(batch, seq, D_MODEL)
    y_sample = _post(xs, s_olat, s_odiff, s_ga, s_gb, post_w, nb).reshape(nb, 1, D_MODEL)

    def kpe_out(t, b, s):
        return jnp.transpose(t, (0, 2, 1)).reshape(1, b, s, MLA_ROPE)

    def dk_out(t, b, s):
        t = t.reshape(b, DIFF_HEADS, 2, DIFF_QK, s)
        return jnp.transpose(t, (0, 4, 1, 2, 3)).reshape(1, b, s, DIFF_HEADS, 2, DIFF_QK)

    return (y_prompt, y_sample,
            p_ckv.reshape(1, batch, seq, MLA_KVR), kpe_out(p_kpe, batch, seq),
            dk_out(p_dk, batch, seq), p_dv.reshape(1, batch, seq, DIFF_HEADS, DIFF_V),
            s_ckv.reshape(1, nb, 1, MLA_KVR), kpe_out(s_kpe, 1, nb).reshape(1, nb, 1, MLA_ROPE),
            dk_out(s_dk, 1, nb).reshape(1, nb, 1, DIFF_HEADS, 2, DIFF_QK),
            s_dv.reshape(1, nb, 1, DIFF_HEADS, DIFF_V))
```

```python
import functools
import math

import numpy as np
import jax
import jax.numpy as jnp
from jax import lax
from jax.experimental import pallas as pl
from jax.experimental.pallas import tpu as pltpu

F32 = jnp.float32
BF16 = jnp.bfloat16

D_MODEL = 1024
PAGE = 128
MLA_HEADS = 8
MLA_NOPE = 64
MLA_ROPE = 32
MLA_V = 64
MLA_QR = 384
MLA_KVR = 128
MLA_SCALE = (MLA_NOPE + MLA_ROPE) ** -0.5
ROPE_THETA = 10000.0
DIFF_HEADS = 4
DIFF_QK = 64
DIFF_V = 128
DIFF_SCALE = DIFF_QK ** -0.5
SUBLN_EPS = 1e-5
REL_BUCKETS = 32
REL_MAX_DIST = 128
FFN_DIM = 2816
NORM_EPS = 1e-6
LAM_INIT = 0.8 - 0.6 * math.exp(-0.3 * 0)

LANES = 128
QCAT = 2 * LANES
NEG = -1e30
LOG2E = math.log2(math.e)
VMEM_LIMIT = 56 * 1024 * 1024

C_QLAT = 0
C_KV = 384
C_KPE = 512
C_KPESW = 640
C_DQ = 768
C_DK = 1280
C_DV = 1792
C_GA = 2304
C_GB = 3328
C_END = 4352

BQ = 256
BK = 256
DEC_PAGES = 16
VT_ROWS = MLA_KVR + 16


def _cparams(sem):
    return pltpu.CompilerParams(dimension_semantics=sem, vmem_limit_bytes=VMEM_LIMIT)


def _const_spec(shape):
    nd = len(shape)
    return pl.BlockSpec(shape, lambda *_: (0,) * nd, pipeline_mode=pl.Buffered(1))


def _rope_tables_kernel(pos_ref, inv_ref, cq_ref, sq_ref, ck_ref, sk_ref):
    ang = pos_ref[...] * inv_ref[...]
    c = jnp.cos(ang)
    s = jnp.sin(ang)
    lane = lax.broadcasted_iota(jnp.int32, ang.shape, 1)
    first_half = (lane & (MLA_ROPE - 1)) < (MLA_ROPE // 2)
    s_signed = jnp.where(first_half, -s, s)
    cq_ref[...] = c
    sq_ref[...] = s_signed
    kmask = lax.broadcasted_iota(jnp.int32, ck_ref.shape, 1) < MLA_ROPE
    ck_ref[...] = jnp.where(kmask, c[:, :LANES], 0.0)
    sk_ref[...] = jnp.where(kmask, s_signed[:, :LANES], 0.0)


def _rope_tables(pos):
    n = pos.shape[0]
    half = MLA_ROPE // 2
    inv = (ROPE_THETA ** (-jnp.arange(half, dtype=F32) / half))
    inv = jnp.tile(inv, 2 * MLA_HEADS).reshape(1, 2 * LANES)
    return pl.pallas_call(
        _rope_tables_kernel,
        out_shape=(jax.ShapeDtypeStruct((n, 2 * LANES), F32), jax.ShapeDtypeStruct((n, 2 * LANES), F32),
                   jax.ShapeDtypeStruct((n, LANES), F32), jax.ShapeDtypeStruct((n, LANES), F32)),
        name="rope_tables",
    )(pos.astype(F32).reshape(n, 1), inv)


def _bucket(n):
    max_exact = REL_BUCKETS // 2
    nf = jnp.maximum(n, 1).astype(F32)
    large = max_exact + (jnp.log(nf / max_exact) / math.log(REL_MAX_DIST / max_exact)
                         * (REL_BUCKETS - max_exact)).astype(jnp.int32)
    large = jnp.minimum(large, REL_BUCKETS - 1)
    return jnp.where(n < max_exact, n, large)


def _bias_kernel(tab_ref, bp_ref, bd_ref, bs_ref, *, past_len):
    kidx = lax.broadcasted_iota(jnp.int32, (BK, BQ), 0)
    qidx = lax.broadcasted_iota(jnp.int32, (BK, BQ), 1)
    for d in range(3):
        rel = d * BQ + qidx - kidx
        b = _bucket(jnp.maximum(rel, 0))
        for h in range(DIFF_HEADS):
            val = jnp.zeros((BK, BQ), F32)
            for t in range(REL_BUCKETS):
                val = jnp.where(b == t, tab_ref[t, h] * LOG2E, val)
            if d == 0:
                val = jnp.where(rel >= 0, val, NEG)
            bp_ref[d * DIFF_HEADS + h] = val
    bp_ref[3 * DIFF_HEADS] = jnp.where(qidx - kidx >= 0, 0.0, NEG)
    gk = DEC_PAGES * PAGE
    row = lax.broadcasted_iota(jnp.int32, (2 * DIFF_HEADS, gk), 0)
    for g in range(past_len // gk):
        kpos = g * gk + lax.broadcasted_iota(jnp.int32, (2 * DIFF_HEADS, gk), 1)
        b = _bucket(jnp.maximum(past_len - kpos, 0))
        val = jnp.zeros((2 * DIFF_HEADS, gk), F32)
        for h in range(DIFF_HEADS):
            for t in range(REL_BUCKETS):
                val = jnp.where((b == t) & ((row >> 1) == h), tab_ref[t, h] * LOG2E, val)
        bd_ref[g] = val
    rows = lax.broadcasted_iota(jnp.int32, (2 * DIFF_HEADS, LANES), 0)
    b0 = _bucket(jnp.zeros((2 * DIFF_HEADS, LANES), jnp.int32))
    vs = jnp.zeros((2 * DIFF_HEADS, LANES), F32)
    for h in range(DIFF_HEADS):
        for t in range(REL_BUCKETS):
            vs = jnp.where((b0 == t) & ((rows >> 1) == h), tab_ref[t, h] * LOG2E, vs)
    bs_ref[...] = vs


def _bias_tables(table, past_len):
    return pl.pallas_call(
        functools.partial(_bias_kernel, past_len=past_len),
        in_specs=[pl.BlockSpec(memory_space=pltpu.SMEM)],
        out_shape=(jax.ShapeDtypeStruct((3 * DIFF_HEADS + 1, BK, BQ), F32),
                   jax.ShapeDtypeStruct((past_len // (DEC_PAGES * PAGE), 2 * DIFF_HEADS, DEC_PAGES * PAGE), F32),
                   jax.ShapeDtypeStruct((2 * DIFF_HEADS, LANES), F32)),
        compiler_params=pltpu.CompilerParams(vmem_limit_bytes=VMEM_LIMIT),
        name="bias_tables",
    )(table.astype(F32))


def _fold_kernel(nope_ref, uk_ref, uv_ref, bm_ref, wfold_ref, wf2_ref):
    for h in range(MLA_HEADS):
        wfold_ref[:, h * LANES:(h + 1) * LANES] = lax.dot_general(
            nope_ref[h], uk_ref[h], (((1,), (1,)), ((), ())),
            preferred_element_type=F32, precision=lax.Precision.HIGHEST)
        wf2_ref[h * LANES:(h + 1) * LANES, :] = jnp.dot(
            uv_ref[h], bm_ref[h], preferred_element_type=F32, precision=lax.Precision.HIGHEST)


def _fold_weights(w_uq, w_uk, w_uv, w_bm):
    uq = w_uq.reshape(MLA_QR, MLA_HEADS, MLA_NOPE + MLA_ROPE)
    nope = jnp.transpose(uq[:, :, :MLA_NOPE], (1, 0, 2))
    uk = jnp.transpose(w_uk, (1, 0, 2))
    uv = jnp.transpose(w_uv, (1, 0, 2))
    bm = w_bm.reshape(MLA_HEADS, MLA_V, D_MODEL)
    return pl.pallas_call(
        _fold_kernel,
        out_shape=(jax.ShapeDtypeStruct((MLA_QR, MLA_HEADS * LANES), F32),
                   jax.ShapeDtypeStruct((MLA_HEADS * LANES, D_MODEL), F32)),
        compiler_params=pltpu.CompilerParams(vmem_limit_bytes=VMEM_LIMIT),
        name="fold_weights",
    )(nope, uk, uv, bm)


def _rms(x, g, eps):
    return x * lax.rsqrt(jnp.mean(x * x, axis=-1, keepdims=True) + eps) * g


def _proj_kernel(x_ref, g_ref, win_ref, gq_ref, wq2_ref, gkv_ref, cq_ref, sq_ref, ck_ref, sk_ref,
                 ckv_ref, kpet_ref, dkt_ref, dv_ref, kvcat_ref, ckvt_ref, qcat_ref, dq_ref, dk16_ref, dvt_ref,
                 ga_ref, gb_ref):
    tm = x_ref.shape[0]
    h = _rms(x_ref[...], g_ref[...], NORM_EPS).astype(BF16)

    def seg(lo, hi):
        return jnp.dot(h, win_ref[:, lo:hi], preferred_element_type=F32)

    qn = _rms(seg(C_QLAT, C_KV), gq_ref[...], NORM_EPS).astype(BF16)
    zq = jnp.dot(qn, wq2_ref[...], preferred_element_type=F32)
    nq = MLA_HEADS * LANES
    qscale = MLA_SCALE * LOG2E
    qpe = (zq[:, nq:nq + 2 * LANES] * cq_ref[...] + zq[:, nq + 2 * LANES:] * sq_ref[...]) * qscale
    lane = lax.broadcasted_iota(jnp.int32, (tm, LANES), 1)
    per_block = LANES // MLA_ROPE
    for hd in range(MLA_HEADS):
        blk = qpe[:, (hd // per_block) * LANES:(hd // per_block + 1) * LANES]
        sh = (hd % per_block) * MLA_ROPE
        if sh:
            blk = pltpu.roll(blk, LANES - sh, axis=1)
        blk = jnp.where(lane < MLA_ROPE, blk, 0.0)
        qcat_ref[:, hd * QCAT:hd * QCAT + LANES] = (zq[:, hd * LANES:(hd + 1) * LANES] * qscale).astype(BF16)
        qcat_ref[:, hd * QCAT + LANES:(hd + 1) * QCAT] = blk.astype(BF16)

    ckv = _rms(seg(C_KV, C_KPE), gkv_ref[...], NORM_EPS)
    kpe = seg(C_KPE, C_KPESW) * ck_ref[...] + seg(C_KPESW, C_DQ) * sk_ref[...]
    ckv_ref[...] = ckv
    kpet_ref[0] = kpe.T[:MLA_ROPE]
    kvcat_ref[:, :LANES] = ckv.astype(BF16)
    kvcat_ref[:, LANES:] = kpe.astype(BF16)
    ones_rows = jnp.ones((VT_ROWS - MLA_KVR, tm), BF16)
    ckvt_ref[0, 0:MLA_KVR, :] = ckv.T.astype(BF16)
    ckvt_ref[0, MLA_KVR:VT_ROWS, :] = ones_rows

    dq_ref[...] = (seg(C_DQ, C_DK) * LOG2E).astype(BF16)
    dk = seg(C_DK, C_DV)
    dkt_ref[0] = dk.T
    dk16_ref[...] = dk.astype(BF16)
    dv = seg(C_DV, C_GA)
    for hd in range(DIFF_HEADS):
        dv_ref[pl.ds(hd, tm, stride=DIFF_HEADS), :] = dv[:, hd * DIFF_V:(hd + 1) * DIFF_V]
    for hd in range(DIFF_HEADS):
        dvt_ref[0, hd, 0:DIFF_V, :] = dv[:, hd * DIFF_V:(hd + 1) * DIFF_V].T.astype(BF16)
        dvt_ref[0, hd, DIFF_V:VT_ROWS, :] = ones_rows

    ga_ref[...] = jax.nn.sigmoid(seg(C_GA, C_GB))
    gb_ref[...] = jax.nn.sigmoid(seg(C_GB, C_END))


def _proj(x, tables, tm, weights):
    g_attn, w_in_r, g_q, w_q2, g_kv = weights
    cq, sq, ck, sk = tables
    n = x.shape[0]
    nb = n // tm
    seq = cq.shape[0]
    tb = seq // tm
    batch = n // seq
    tcol = lambda i: (i // tb, 0, i % tb)
    row = lambda i: (i, 0)
    trow = lambda i: (i % tb, 0)
    in_specs = [
        pl.BlockSpec((tm, D_MODEL), row),
        _const_spec((1, D_MODEL)),
        _const_spec((D_MODEL, C_END)),
        _const_spec((1, MLA_QR)),
        _const_spec((MLA_QR, w_q2.shape[1])),
        _const_spec((1, MLA_KVR)),
        pl.BlockSpec((tm, 2 * LANES), trow), pl.BlockSpec((tm, 2 * LANES), trow),
        pl.BlockSpec((tm, LANES), trow), pl.BlockSpec((tm, LANES), trow),
    ]
    out_shape = (
        jax.ShapeDtypeStruct((n, MLA_KVR), F32),
        jax.ShapeDtypeStruct((batch, MLA_ROPE, seq), F32),
        jax.ShapeDtypeStruct((batch, 2 * DIFF_HEADS * DIFF_QK, seq), F32),
        jax.ShapeDtypeStruct((n * DIFF_HEADS, DIFF_V), F32),
        jax.ShapeDtypeStruct((n, QCAT), BF16),
        jax.ShapeDtypeStruct((nb, VT_ROWS, tm), BF16),
        jax.ShapeDtypeStruct((n, MLA_HEADS * QCAT), BF16),
        jax.ShapeDtypeStruct((n, 2 * DIFF_HEADS * DIFF_QK), BF16),
        jax.ShapeDtypeStruct((n, 2 * DIFF_HEADS * DIFF_QK), BF16),
        jax.ShapeDtypeStruct((nb, DIFF_HEADS, VT_ROWS, tm), BF16),
        jax.ShapeDtypeStruct((n, D_MODEL), F32),
        jax.ShapeDtypeStruct((n, D_MODEL), F32),
    )
    out_specs = (
        pl.BlockSpec((tm, MLA_KVR), row),
        pl.BlockSpec((1, MLA_ROPE, tm), tcol),
        pl.BlockSpec((1, 512, tm), tcol),
        pl.BlockSpec((tm * DIFF_HEADS, DIFF_V), row),
        pl.BlockSpec((tm, QCAT), row),
        pl.BlockSpec((1, VT_ROWS, tm), lambda i: (i, 0, 0)),
        pl.BlockSpec((tm, MLA_HEADS * QCAT), row),
        pl.BlockSpec((tm, 512), row),
        pl.BlockSpec((tm, 512), row),
        pl.BlockSpec((1, DIFF_HEADS, VT_ROWS, tm), lambda i: (i, 0, 0, 0)),
        pl.BlockSpec((tm, D_MODEL), row),
        pl.BlockSpec((tm, D_MODEL), row),
    )
    return pl.pallas_call(
        _proj_kernel,
        grid=(nb,),
        in_specs=in_specs,
        out_specs=out_specs,
        out_shape=out_shape,
        compiler_params=_cparams(("parallel",)),
        name="proj_in",
    )(x, g_attn, w_in_r, g_q, w_q2, g_kv, cq, sq, ck, sk)


def _lambda(lq1_ref, lk1_ref, lq2_ref, lk2_ref):
    a = jnp.sum(lq1_ref[...] * lk1_ref[...], axis=-1, keepdims=True)
    b = jnp.sum(lq2_ref[...] * lk2_ref[...], axis=-1, keepdims=True)
    return jnp.exp(a) - jnp.exp(b) + LAM_INIT


NQ_MLA = MLA_HEADS * BQ
NQ_DIFF = 2 * BQ
NQ_ALL = NQ_MLA + DIFF_HEADS * NQ_DIFF
GROUP_KEYS = DEC_PAGES * PAGE
N_SLOTS = 2


def _fused_kernel(pt_ref,
                  qcat_ref, dq_ref, kv_ref, ckvt_ref, dk_ref, dvt_ref, bias_ref,
                  lq1_ref, lk1_ref, lq2_ref, lk2_ref, subg_ref, subgr_ref,
                  sq_ref, sdq_ref, skv_ref, sdk_ref, sdv_ref, bd_ref, bs_ref,
                  cckv_hbm, ckpe_hbm, cdk_hbm, cdv_hbm,
                  olat_ref, odiff_ref, solat_ref, sodiff_ref,
                  qall_scr, qd_scr, m_scr, acc_scr,
                  ckv_buf, kpe_buf, dkt_buf, dv_buf, sem,
                  smm_scr, slm_scr, sam_scr, smd_scr, sld_scr, sad_scr,
                  *, n_pages, rows_per_step):
    b = pl.program_id(0)
    i = pl.program_id(1)
    nq = pl.num_programs(1)
    step = b * nq + i
    n_steps = pl.num_programs(0) * nq
    groups_per_row = n_pages // DEC_PAGES
    n_groups = rows_per_step * groups_per_row
    nrow = 2 * DIFF_HEADS

    def group_copies(row, grp, slot):
        cps = []
        for k in range(DEC_PAGES):
            pg = pt_ref[row * n_pages + grp * DEC_PAGES + k]
            keys = pl.ds(k * PAGE, PAGE)
            cps.append(pltpu.make_async_copy(cckv_hbm.at[pg], ckv_buf.at[slot, keys, :], sem.at[slot, 0]))
            cps.append(pltpu.make_async_copy(ckpe_hbm.at[pg], kpe_buf.at[slot, :, keys], sem.at[slot, 1]))
            cps.append(pltpu.make_async_copy(cdk_hbm.at[pg], dkt_buf.at[slot, :, keys], sem.at[slot, 2]))
            cps.append(pltpu.make_async_copy(cdv_hbm.at[pg],
                                             dv_buf.at[slot, pl.ds(k * PAGE * DIFF_HEADS, PAGE * DIFF_HEADS), :],
                                             sem.at[slot, 3]))
        return cps

    def start_group(row, grp, slot):
        for c in group_copies(row, grp, slot):
            c.start()

    def wait_group(row, grp, slot):
        for c in group_copies(row, grp, slot):
            c.wait()

    @pl.when(step == 0)
    def _():
        start_group(0, 0, 0)

    def decode_group(g):
        r = g // groups_per_row
        grp = g % groups_per_row
        slot = g % N_SLOTS
        row = step * rows_per_step + r
        wait_group(row, grp, slot)

        gg = jnp.minimum(step * n_groups + g + 1, n_steps * n_groups - 1)
        start_group(gg // groups_per_row, gg % groups_per_row, (g + 1) % N_SLOTS)

        q = sq_ref[r]
        qa = q[:, :MLA_KVR]
        qp = q[:, MLA_KVR:MLA_KVR + MLA_ROPE]
        rowi = lax.broadcasted_iota(jnp.int32, (nrow, 512), 0)
        lanei = lax.broadcasted_iota(jnp.int32, (nrow, 512), 1)
        dq_b = jnp.broadcast_to(sdq_ref[r].astype(F32), (nrow, 512))
        qbd = jnp.where((lanei >> 6) == rowi, dq_b, 0.0).astype(BF16)

        first = grp == 0
        kvs = skv_ref[r].astype(F32)
        smm_scr[...] = jnp.where(first, jnp.sum(q.astype(F32) * kvs, axis=1, keepdims=True), smm_scr[...])
        slm_scr[...] = jnp.where(first, 1.0, slm_scr[...])
        sam_scr[...] = jnp.where(first, jnp.broadcast_to(kvs[:, :MLA_KVR], sam_scr.shape), sam_scr[...])
        s0 = jnp.sum(qbd.astype(F32) * sdk_ref[r].astype(F32), axis=1, keepdims=True)
        smd_scr[...] = jnp.where(first, s0 + bs_ref[:, 0:1], smd_scr[...])
        sld_scr[...] = jnp.where(first, 1.0, sld_scr[...])
        sad_scr[...] = jnp.where(first, jnp.broadcast_to(sdv_ref[r].astype(F32), sad_scr.shape), sad_scr[...])

        tdims = (((1,), (1,)), ((), ()))
        ckv16 = ckv_buf[slot].astype(BF16)
        sm = (lax.dot_general(qa, ckv16, tdims, preferred_element_type=F32)
              + jnp.dot(qp, kpe_buf[slot].astype(BF16), preferred_element_type=F32))
        sd = jnp.dot(qbd, dkt_buf[slot].astype(BF16), preferred_element_type=F32) + bd_ref[grp]

        def merge(s, m_scr, l_scr):
            m_old = m_scr[...]
            m_new = jnp.maximum(m_old, jnp.max(s, axis=1, keepdims=True))
            a = jnp.exp2(m_old - m_new)
            p = jnp.exp2(s - m_new)
            l_scr[...] = l_scr[...] * a + jnp.sum(p, axis=1, keepdims=True)
            m_scr[...] = m_new
            return a, p.astype(BF16)

        a, p = merge(sm, smm_scr, slm_scr)
        sam_scr[...] = sam_scr[...] * a + jnp.dot(p, ckv16, preferred_element_type=F32)
        a, p = merge(sd, smd_scr, sld_scr)
        v_all = jnp.concatenate(
            [dv_buf[slot, pl.ds(hd, GROUP_KEYS, stride=DIFF_HEADS), :].astype(BF16) for hd in range(DIFF_HEADS)],
            axis=1)
        sad_scr[...] = sad_scr[...] * a + jnp.dot(p, v_all, preferred_element_type=F32)

        solat_ref[r] = sam_scr[...] / slm_scr[...]
        lam = _lambda(lq1_ref, lk1_ref, lq2_ref, lk2_ref)
        o = sad_scr[...] / sld_scr[...]
        for hd in range(DIFF_HEADS):
            o1 = o[2 * hd:2 * hd + 1, hd * DIFF_V:(hd + 1) * DIFF_V]
            o2 = o[2 * hd + 1:2 * hd + 2, hd * DIFF_V:(hd + 1) * DIFF_V]
            y = _rms(o1 - lam * o2, subgr_ref[...], SUBLN_EPS) * (1.0 - LAM_INIT)
            sodiff_ref[r, hd:hd + 1, :] = y

    for hd in range(MLA_HEADS):
        qall_scr[hd * BQ:(hd + 1) * BQ, :] = qcat_ref[:, hd * QCAT:(hd + 1) * QCAT]
    lane = lax.broadcasted_iota(jnp.int32, (BQ, LANES), 1)
    for hd in range(DIFF_HEADS):
        dqh = dq_ref[:, hd * LANES:(hd + 1) * LANES]
        qd_scr[hd, 0:BQ, :] = jnp.where(lane < DIFF_QK, dqh, jnp.zeros_like(dqh))
        qd_scr[hd, BQ:2 * BQ, :] = jnp.where(lane >= DIFF_QK, dqh, jnp.zeros_like(dqh))

    m_scr[...] = jnp.full(m_scr.shape, NEG, F32)
    acc_scr[...] = jnp.zeros(acc_scr.shape, F32)
    tdims = (((1,), (1,)), ((), ()))

    def block(j, diag):
        koff = pl.multiple_of(j * BK, BK)
        st = lax.dot_general(kv_ref[pl.ds(koff, BK), :], qall_scr[...], tdims, preferred_element_type=F32)
        if diag:
            mask = bias_ref[3 * DIFF_HEADS]
            st = st + jnp.concatenate([mask] * MLA_HEADS, axis=1)
        d = 0 if diag else jnp.minimum(i - j, 2)
        parts = [st]
        for hd in range(DIFF_HEADS):
            kb = dk_ref[pl.ds(koff, BK), hd * LANES:(hd + 1) * LANES]
            b1 = bias_ref[d * DIFF_HEADS + hd]
            sh = lax.dot_general(kb, qd_scr[hd], tdims, preferred_element_type=F32)
            parts.append(sh + jnp.concatenate([b1, b1], axis=1))
        st = jnp.concatenate(parts, axis=1)
        m_old = m_scr[...]
        m_new = jnp.maximum(m_old, jnp.max(st, axis=0, keepdims=True))
        a = jnp.exp2(m_old - m_new)
        pt = jnp.exp2(st - m_new).astype(BF16)
        m_scr[...] = m_new
        cs = slice(0, NQ_MLA)
        acc_scr[:, cs] = acc_scr[:, cs] * a[:, cs] + jnp.dot(ckvt_ref[j], pt[:, cs], preferred_element_type=F32)
        for hd in range(DIFF_HEADS):
            cs = slice(NQ_MLA + hd * NQ_DIFF, NQ_MLA + (hd + 1) * NQ_DIFF)
            acc_scr[:, cs] = acc_scr[:, cs] * a[:, cs] + jnp.dot(dvt_ref[j, hd], pt[:, cs],
                                                               preferred_element_type=F32)

    def paired(j, carry):
        decode_group(j)
        block(j, False)
        return carry

    def alone(j, carry):
        block(j, False)
        return carry

    def tail(j, carry):
        decode_group(j)
        return carry

    n_paired = jnp.minimum(i, n_groups)
    lax.fori_loop(0, n_paired, paired, 0)
    lax.fori_loop(n_paired, i, alone, 0)

    @pl.when(i < n_groups)
    def _():
        decode_group(i)

    block(i, True)
    lax.fori_loop(i + 1, n_groups, tail, 0)

    @pl.when(step == n_steps - 1)
    def _():
        last = n_steps * n_groups - 1
        wait_group(last // groups_per_row, last % groups_per_row, n_groups % N_SLOTS)

    acc = acc_scr[...]
    o = acc[:MLA_KVR] / acc[MLA_KVR:MLA_KVR + 1]
    for hd in range(MLA_HEADS):
        olat_ref[:, hd * LANES:(hd + 1) * LANES] = o[:, hd * BQ:(hd + 1) * BQ].T.astype(BF16)
    lam = _lambda(lq1_ref, lk1_ref, lq2_ref, lk2_ref)
    for hd in range(DIFF_HEADS):
        c0 = NQ_MLA + hd * NQ_DIFF
        od = o[:, c0:c0 + BQ] - lam * o[:, c0 + BQ:c0 + 2 * BQ]
        y = od * lax.rsqrt(jnp.mean(od * od, axis=0, keepdims=True) + SUBLN_EPS) * subg_ref[...]
        odiff_ref[:, hd * LANES:(hd + 1) * LANES] = (y * (1.0 - LAM_INIT)).T.astype(BF16)


def _attn_fused(p_in, s_in, caches, page_table, bias, bias_dec, bias_self, lam_vecs, subg, batch, seq):
    qcat, dq, kvcat, ckvt, dk16, dvt = p_in
    s_qcat, s_dq, s_kvcat, s_dk16, s_dv16 = s_in
    nq = seq // BQ
    nk = seq // BK
    n = batch * seq
    nb, n_pages = page_table.shape
    n_steps = batch * nq
    assert nb % n_steps == 0 and n_pages % DEC_PAGES == 0
    rps = nb // n_steps
    n_groups = rps * (n_pages // DEC_PAGES)
    assert n_groups % N_SLOTS == 0
    pt = page_table.reshape(-1).astype(jnp.int32)

    qrow = lambda b, i, pt: (b * nq + i, 0)
    srow = lambda b, i, pt: (b * nq + i, 0, 0)

    def const(shape):
        nd = len(shape)
        return pl.BlockSpec(shape, lambda b, i, pt: (0,) * nd, pipeline_mode=pl.Buffered(1))

    in_specs = [
        pl.BlockSpec((BQ, MLA_HEADS * QCAT), qrow),
        pl.BlockSpec((BQ, 512), qrow),
        pl.BlockSpec((seq, QCAT), lambda b, i, pt: (b, 0), pipeline_mode=pl.Buffered(1)),
        pl.BlockSpec((nk, VT_ROWS, BK), lambda b, i, pt: (b, 0, 0), pipeline_mode=pl.Buffered(1)),
        pl.BlockSpec((seq, 512), lambda b, i, pt: (b, 0), pipeline_mode=pl.Buffered(1)),
        pl.BlockSpec((nk, DIFF_HEADS, VT_ROWS, BK), lambda b, i, pt: (b, 0, 0, 0), pipeline_mode=pl.Buffered(1)),
        const(bias.shape),
        const((1, DIFF_QK)), const((1, DIFF_QK)), const((1, DIFF_QK)), const((1, DIFF_QK)),
        const((DIFF_V, 1)), const((1, DIFF_V)),
        pl.BlockSpec((rps, MLA_HEADS, QCAT), srow),
        pl.BlockSpec((rps, 1, 512), srow),
        pl.BlockSpec((rps, 1, QCAT), srow),
        pl.BlockSpec((rps, 1, 512), srow),
        pl.BlockSpec((rps, 1, 512), srow),
        const(bias_dec.shape),
        const(bias_self.shape),
        pl.BlockSpec(memory_space=pl.ANY), pl.BlockSpec(memory_space=pl.ANY),
        pl.BlockSpec(memory_space=pl.ANY), pl.BlockSpec(memory_space=pl.ANY),
    ]
    out_specs = (pl.BlockSpec((BQ, MLA_HEADS * LANES), qrow), pl.BlockSpec((BQ, DIFF_HEADS * DIFF_V), qrow),
                 pl.BlockSpec((rps, MLA_HEADS, LANES), srow), pl.BlockSpec((rps, DIFF_HEADS, DIFF_V), srow))
    out_shape = (jax.ShapeDtypeStruct((n, MLA_HEADS * LANES), BF16),
                 jax.ShapeDtypeStruct((n, DIFF_HEADS * DIFF_V), BF16),
                 jax.ShapeDtypeStruct((nb, MLA_HEADS, LANES), F32),
                 jax.ShapeDtypeStruct((nb, DIFF_HEADS, DIFF_V), F32))
    nrow = 2 * DIFF_HEADS
    scratch = [
        pltpu.VMEM((NQ_MLA, QCAT), BF16),
        pltpu.VMEM((DIFF_HEADS, NQ_DIFF, LANES), BF16),
        pltpu.VMEM((1, NQ_ALL), F32), pltpu.VMEM((VT_ROWS, NQ_ALL), F32),
        pltpu.VMEM((N_SLOTS, GROUP_KEYS, MLA_KVR), F32),
        pltpu.VMEM((N_SLOTS, MLA_ROPE, GROUP_KEYS), F32),
        pltpu.VMEM((N_SLOTS, 512, GROUP_KEYS), F32),
        pltpu.VMEM((N_SLOTS, GROUP_KEYS * DIFF_HEADS, DIFF_V), F32),
        pltpu.SemaphoreType.DMA((N_SLOTS, 4)),
        pltpu.VMEM((MLA_HEADS, 1), F32), pltpu.VMEM((MLA_HEADS, 1), F32), pltpu.VMEM((MLA_HEADS, MLA_KVR), F32),
        pltpu.VMEM((nrow, 1), F32), pltpu.VMEM((nrow, 1), F32), pltpu.VMEM((nrow, 512), F32),
    ]
    grid_spec = pltpu.PrefetchScalarGridSpec(
        num_scalar_prefetch=1, grid=(batch, nq), in_specs=in_specs, out_specs=out_specs, scratch_shapes=scratch)
    olat, odiff, s_olat, s_odiff = pl.pallas_call(
        functools.partial(_fused_kernel, n_pages=n_pages, rows_per_step=rps),
        grid_spec=grid_spec,
        out_shape=out_shape,
        compiler_params=_cparams(("arbitrary", "arbitrary")),
        name="attn_fused",
    )(pt, qcat, dq, kvcat, ckvt, dk16, dvt, bias, *lam_vecs, subg.reshape(-1, 1), subg.reshape(1, -1),
      s_qcat.reshape(nb, MLA_HEADS, QCAT), s_dq.reshape(nb, 1, 512), s_kvcat.reshape(nb, 1, QCAT),
      s_dk16.reshape(nb, 1, 512), s_dv16.reshape(nb, 1, 512), bias_dec, bias_self, *caches)
    return (olat, odiff, s_olat.reshape(nb, MLA_HEADS * LANES).astype(BF16),
            s_odiff.reshape(nb, DIFF_HEADS * DIFF_V).astype(BF16))


def _post_kernel(x_ref, olat_ref, odiff_ref, ga_ref, gb_ref, wf2_ref, wbd_ref, wo_ref,
                 gffn_ref, wg_ref, wu_ref, wd_ref, gfin_ref, y_ref):
    bm = jnp.dot(olat_ref[...], wf2_ref[...], preferred_element_type=F32)
    bd = jnp.dot(odiff_ref[...], wbd_ref[...], preferred_element_type=F32)
    merged = (ga_ref[...] * bm + gb_ref[...] * bd).astype(BF16)
    x = x_ref[...] + jnp.dot(merged, wo_ref[...], preferred_element_type=F32)
    h2 = _rms(x, gffn_ref[...], NORM_EPS).astype(BF16)
    gate = jnp.dot(h2, wg_ref[...], preferred_element_type=F32)
    up = jnp.dot(h2, wu_ref[...], preferred_element_type=F32)
    act = (gate * jax.nn.sigmoid(gate) * up).astype(BF16)
    x = x + jnp.dot(act, wd_ref[...], preferred_element_type=F32)
    y_ref[...] = _rms(x, gfin_ref[...], NORM_EPS)


def _post(x, olat, odiff, ga, gb, weights, tm):
    n = x.shape[0]
    row = lambda i: (i, 0)
    wf2, wbd, wo, gffn, wg, wu, wd, gfin = weights
    single = pl.Buffered(1)

    def wspec(shape):
        return pl.BlockSpec(shape, lambda i: (0, 0), pipeline_mode=single)

    in_specs = [
        pl.BlockSpec((tm, D_MODEL), row),
        pl.BlockSpec((tm, MLA_HEADS * LANES), row),
        pl.BlockSpec((tm, 512), row),
        pl.BlockSpec((tm, D_MODEL), row),
        pl.BlockSpec((tm, D_MODEL), row),
        wspec(wf2.shape), wspec(wbd.shape), wspec(wo.shape), wspec(gffn.shape),
        wspec(wg.shape), wspec(wu.shape), wspec(wd.shape), wspec(gfin.shape),
    ]
    return pl.pallas_call(
        _post_kernel,
        grid=(n // tm,),
        in_specs=in_specs,
        out_specs=pl.BlockSpec((tm, D_MODEL), row),
        out_shape=jax.ShapeDtypeStruct((n, D_MODEL), F32),
        compiler_params=_cparams(("parallel",)),
        name="merge_ffn",
    )(x, olat, odiff, ga, gb, wf2, wbd, wo, gffn, wg, wu, wd, gfin)


def kernel(x_prompt, x_sample, cache_mla_ckv, cache_mla_kpe, cache_diff_k, cache_diff_v, page_table, rel_bias_table, norm_attn_g, w_in, mla_q_norm_g, mla_w_uq, mla_kv_norm_g, mla_w_uk, mla_w_uv, diff_lambda_q1, diff_lambda_k1, diff_lambda_q2, diff_lambda_k2, diff_subln_g, w_branch_mla, w_branch_diff, w_o, norm_ffn_g, w_ffn_gate, w_ffn_up, w_ffn_down, norm_final_g):
    batch, seq, _ = x_prompt.shape
    nb, dec_seq, _ = x_sample.shape
    assert dec_seq == 1 and w_in.shape[0] == 1
    n_pages = page_table.shape[1]
    past_len = n_pages * PAGE
    assert REL_MAX_DIST <= BQ + 1 and seq % BQ == 0
    n_pool = cache_mla_ckv.shape[1]

    w = w_in[0]
    sw = np.concatenate([np.arange(16, 32), np.arange(0, 16)])
    z96 = jnp.zeros((D_MODEL, LANES - MLA_ROPE), F32)
    w_kpe = w[:, 512:544]
    w_in_r = jnp.concatenate(
        [w[:, 0:384], w[:, 384:512], w_kpe, z96, w_kpe[:, sw], z96,
         w[:, 544:1056] * DIFF_SCALE, w[:, 1056:1568], w[:, 1568:2080], w[:, 2080:3104], w[:, 3104:4128]],
        axis=1).astype(BF16)
    w_fold, w_f2 = _fold_weights(mla_w_uq[0], mla_w_uk[0], mla_w_uv[0], w_branch_mla[0])
    uq = mla_w_uq[0].reshape(MLA_QR, MLA_HEADS, MLA_NOPE + MLA_ROPE)
    w_pe = uq[:, :, MLA_NOPE:]
    w_q2 = jnp.concatenate([w_fold, w_pe.reshape(MLA_QR, -1), w_pe[:, :, sw].reshape(MLA_QR, -1)],
                           axis=1).astype(BF16)
    proj_w = (norm_attn_g[0].reshape(1, -1), w_in_r, mla_q_norm_g[0].reshape(1, -1), w_q2,
              mla_kv_norm_g[0].reshape(1, -1))
    post_w = (w_f2.astype(BF16), w_branch_diff[0].astype(BF16), w_o[0].astype(BF16),
              norm_ffn_g[0].reshape(1, -1), w_ffn_gate[0].astype(BF16), w_ffn_up[0].astype(BF16),
              w_ffn_down[0].astype(BF16), norm_final_g.reshape(1, -1))
    lam_vecs = tuple(v[0].reshape(1, -1) for v in (diff_lambda_q1, diff_lambda_k1, diff_lambda_q2, diff_lambda_k2))
    subg = diff_subln_g[0]

    tab_p = _rope_tables(jnp.arange(seq, dtype=jnp.int32))
    tab_s = _rope_tables(jnp.full((nb,), past_len, dtype=jnp.int32))
    bias_p, bias_dec, bias_self = _bias_tables(rel_bias_table, past_len)

    xp = x_prompt.reshape(batch * seq, D_MODEL)
    (p_ckv, p_kpe, p_dk, p_dv, p_kvcat, p_ckvt, p_qcat, p_dq, p_dk16, p_dvt, p_ga, p_gb) = _proj(
        xp, tab_p, BK, proj_w)
    xs = x_sample.reshape(nb, D_MODEL)
    (s_ckv, s_kpe, s_dk, s_dv, s_kvcat, _, s_qcat, s_dq, s_dk16, s_dvt, s_ga, s_gb) = _proj(
        xs, tab_s, nb, proj_w)

    caches = (cache_mla_ckv.reshape(n_pool, PAGE, MLA_KVR),
              jnp.transpose(cache_mla_kpe, (0, 1, 3, 2)).reshape(n_pool, MLA_ROPE, PAGE),
              jnp.transpose(cache_diff_k, (0, 1, 3, 4, 5, 2)).reshape(n_pool, 512, PAGE),
              cache_diff_v.reshape(n_pool, PAGE * DIFF_HEADS, DIFF_V))
    p_olat, p_odiff, s_olat, s_odiff = _attn_fused(
        (p_qcat, p_dq, p_kvcat, p_ckvt, p_dk16, p_dvt),
        (s_qcat, s_dq, s_kvcat, s_dk16, s_dv.reshape(nb, DIFF_HEADS * DIFF_V).astype(BF16)),
        caches, page_table, bias_p, bias_dec, bias_self, lam_vecs, subg, batch, seq)

    y_prompt = _post(xp, p_olat, p_odiff, p_ga, p_gb, post_w, 256).reshape(batch, seq, D_MODEL)
    y_sample = _post(xs, s_olat, s_odiff, s_ga, s_gb, post_w, nb).reshape(nb, 1, D_MODEL)

    def kpe_out(t, b, s):
        return jnp.transpose(t, (0, 2, 1)).reshape(1, b, s, MLA_ROPE)

    def dk_out(t, b, s):
        t = t.reshape(b, DIFF_HEADS, 2, DIFF_QK, s)
        return jnp.transpose(t, (0, 4, 1, 2, 3)).reshape(1, b, s, DIFF_HEADS, 2, DIFF_QK)

    return (y_prompt, y_sample,
            p_ckv.reshape(1, batch, seq, MLA_KVR), kpe_out(p_kpe, batch, seq),
            dk_out(p_dk, batch, seq), p_dv.reshape(1, batch, seq, DIFF_HEADS, DIFF_V),
            s_ckv.reshape(1, nb, 1, MLA_KVR), kpe_out(s_kpe, 1, nb).reshape(1, nb, 1, MLA_ROPE),
            dk_out(s_dk, 1, nb).reshape(1, nb, 1, DIFF_HEADS, 2, DIFF_QK),
            s_dv.reshape(1, nb, 1, DIFF_HEADS, DIFF_V))
```

```python
import functools
import math

import numpy as np
import jax
import jax.numpy as jnp
from jax import lax
from jax.experimental import pallas as pl
from jax.experimental.pallas import tpu as pltpu

F32 = jnp.float32
BF16 = jnp.bfloat16

D_MODEL = 1024
PAGE = 128
MLA_HEADS = 8
MLA_NOPE = 64
MLA_ROPE = 32
MLA_V = 64
MLA_QR = 384
MLA_KVR = 128
MLA_SCALE = (MLA_NOPE + MLA_ROPE) ** -0.5
ROPE_THETA = 10000.0
DIFF_HEADS = 4
DIFF_QK = 64
DIFF_V = 128
DIFF_SCALE = DIFF_QK ** -0.5
SUBLN_EPS = 1e-5
REL_BUCKETS = 32
REL_MAX_DIST = 128
FFN_DIM = 2816
NORM_EPS = 1e-6
LAM_INIT = 0.8 - 0.6 * math.exp(-0.3 * 0)

LANES = 128
QCAT = 2 * LANES
NEG = -1e30
LOG2E = math.log2(math.e)
VMEM_LIMIT = 56 * 1024 * 1024

C_QLAT = 0
C_KV = 384
C_KPE = 512
C_KPESW = 640
C_DQ = 768
C_DK = 1280
C_DV = 1792
C_GA = 2304
C_GB = 3328
C_END = 4352

BQ = 256
BK = 256
DEC_PAGES = 16
VT_ROWS = MLA_KVR + 16


def _cparams(sem):
    return pltpu.CompilerParams(dimension_semantics=sem, vmem_limit_bytes=VMEM_LIMIT)


def _const_spec(shape):
    nd = len(shape)
    return pl.BlockSpec(shape, lambda *_: (0,) * nd, pipeline_mode=pl.Buffered(1))


def _rope_tables_kernel(pos_ref, inv_ref, cq_ref, sq_ref, ck_ref, sk_ref):
    ang = pos_ref[...] * inv_ref[...]
    c = jnp.cos(ang)
    s = jnp.sin(ang)
    lane = lax.broadcasted_iota(jnp.int32, ang.shape, 1)
    first_half = (lane & (MLA_ROPE - 1)) < (MLA_ROPE // 2)
    s_signed = jnp.where(first_half, -s, s)
    cq_ref[...] = c
    sq_ref[...] = s_signed
    kmask = lax.broadcasted_iota(jnp.int32, ck_ref.shape, 1) < MLA_ROPE
    ck_ref[...] = jnp.where(kmask, c[:, :LANES], 0.0)
    sk_ref[...] = jnp.where(kmask, s_signed[:, :LANES], 0.0)


def _rope_tables(pos):
    n = pos.shape[0]
    half = MLA_ROPE // 2
    inv = (ROPE_THETA ** (-jnp.arange(half, dtype=F32) / half))
    inv = jnp.tile(inv, 2 * MLA_HEADS).reshape(1, 2 * LANES)
    return pl.pallas_call(
        _rope_tables_kernel,
        out_shape=(jax.ShapeDtypeStruct((n, 2 * LANES), F32), jax.ShapeDtypeStruct((n, 2 * LANES), F32),
                   jax.ShapeDtypeStruct((n, LANES), F32), jax.ShapeDtypeStruct((n, LANES), F32)),
        name="rope_tables",
    )(pos.astype(F32).reshape(n, 1), inv)


def _bucket(n):
    max_exact = REL_BUCKETS // 2
    nf = jnp.maximum(n, 1).astype(F32)
    large = max_exact + (jnp.log(nf / max_exact) / math.log(REL_MAX_DIST / max_exact)
                         * (REL_BUCKETS - max_exact)).astype(jnp.int32)
    large = jnp.minimum(large, REL_BUCKETS - 1)
    return jnp.where(n < max_exact, n, large)


def _bias_kernel(tab_ref, bp_ref, bd_ref, bs_ref, *, past_len):
    kidx = lax.broadcasted_iota(jnp.int32, (BK, BQ), 0)
    qidx = lax.broadcasted_iota(jnp.int32, (BK, BQ), 1)
    for d in range(3):
        rel = d * BQ + qidx - kidx
        b = _bucket(jnp.maximum(rel, 0))
        for h in range(DIFF_HEADS):
            val = jnp.zeros((BK, BQ), F32)
            for t in range(REL_BUCKETS):
                val = jnp.where(b == t, tab_ref[t, h] * LOG2E, val)
            if d == 0:
                val = jnp.where(rel >= 0, val, NEG)
            bp_ref[d * DIFF_HEADS + h] = val
    bp_ref[3 * DIFF_HEADS] = jnp.where(qidx - kidx >= 0, 0.0, NEG)
    gk = DEC_PAGES * PAGE
    row = lax.broadcasted_iota(jnp.int32, (2 * DIFF_HEADS, gk), 0)
    for g in range(past_len // gk):
        kpos = g * gk + lax.broadcasted_iota(jnp.int32, (2 * DIFF_HEADS, gk), 1)
        b = _bucket(jnp.maximum(past_len - kpos, 0))
        val = jnp.zeros((2 * DIFF_HEADS, gk), F32)
        for h in range(DIFF_HEADS):
            for t in range(REL_BUCKETS):
                val = jnp.where((b == t) & ((row >> 1) == h), tab_ref[t, h] * LOG2E, val)
        bd_ref[g] = val
    rows = lax.broadcasted_iota(jnp.int32, (2 * DIFF_HEADS, LANES), 0)
    b0 = _bucket(jnp.zeros((2 * DIFF_HEADS, LANES), jnp.int32))
    vs = jnp.zeros((2 * DIFF_HEADS, LANES), F32)
    for h in range(DIFF_HEADS):
        for t in range(REL_BUCKETS):
            vs = jnp.where((b0 == t) & ((rows >> 1) == h), tab_ref[t, h] * LOG2E, vs)
    bs_ref[...] = vs


def _bias_tables(table, past_len):
    return pl.pallas_call(
        functools.partial(_bias_kernel, past_len=past_len),
        in_specs=[pl.BlockSpec(memory_space=pltpu.SMEM)],
        out_shape=(jax.ShapeDtypeStruct((3 * DIFF_HEADS + 1, BK, BQ), F32),
                   jax.ShapeDtypeStruct((past_len // (DEC_PAGES * PAGE), 2 * DIFF_HEADS, DEC_PAGES * PAGE), F32),
                   jax.ShapeDtypeStruct((2 * DIFF_HEADS, LANES), F32)),
        compiler_params=pltpu.CompilerParams(vmem_limit_bytes=VMEM_LIMIT),
        name="bias_tables",
    )(table.astype(F32))


def _fold_kernel(nope_ref, uk_ref, uv_ref, bm_ref, wfold_ref, wf2_ref):
    for h in range(MLA_HEADS):
        wfold_ref[:, h * LANES:(h + 1) * LANES] = lax.dot_general(
            nope_ref[h], uk_ref[h], (((1,), (1,)), ((), ())),
            preferred_element_type=F32, precision=lax.Precision.HIGHEST)
        wf2_ref[h * LANES:(h + 1) * LANES, :] = jnp.dot(
            uv_ref[h], bm_ref[h], preferred_element_type=F32, precision=lax.Precision.HIGHEST)


def _fold_weights(w_uq, w_uk, w_uv, w_bm):
    uq = w_uq.reshape(MLA_QR, MLA_HEADS, MLA_NOPE + MLA_ROPE)
    nope = jnp.transpose(uq[:, :, :MLA_NOPE], (1, 0, 2))
    uk = jnp.transpose(w_uk, (1, 0, 2))
    uv = jnp.transpose(w_uv, (1, 0, 2))
    bm = w_bm.reshape(MLA_HEADS, MLA_V, D_MODEL)
    return pl.pallas_call(
        _fold_kernel,
        out_shape=(jax.ShapeDtypeStruct((MLA_QR, MLA_HEADS * LANES), F32),
                   jax.ShapeDtypeStruct((MLA_HEADS * LANES, D_MODEL), F32)),
        compiler_params=pltpu.CompilerParams(vmem_limit_bytes=VMEM_LIMIT),
        name="fold_weights",
    )(nope, uk, uv, bm)


def _rms(x, g, eps):
    return x * lax.rsqrt(jnp.mean(x * x, axis=-1, keepdims=True) + eps) * g


def _proj_kernel(x_ref, g_ref, win_ref, gq_ref, wq2_ref, gkv_ref, cq_ref, sq_ref, ck_ref, sk_ref,
                 ckv_ref, kpet_ref, dkt_ref, dv_ref, kvcat_ref, ckvt_ref, qcat_ref, dq_ref, dk16_ref, dvt_ref,
                 ga_ref, gb_ref):
    tm = x_ref.shape[0]
    h = _rms(x_ref[...], g_ref[...], NORM_EPS).astype(BF16)

    def seg(lo, hi):
        return jnp.dot(h, win_ref[:, lo:hi], preferred_element_type=F32)

    qn = _rms(seg(C_QLAT, C_KV), gq_ref[...], NORM_EPS).astype(BF16)
    zq = jnp.dot(qn, wq2_ref[...], preferred_element_type=F32)
    nq = MLA_HEADS * LANES
    qscale = MLA_SCALE * LOG2E
    qpe = (zq[:, nq:nq + 2 * LANES] * cq_ref[...] + zq[:, nq + 2 * LANES:] * sq_ref[...]) * qscale
    lane = lax.broadcasted_iota(jnp.int32, (tm, LANES), 1)
    per_block = LANES // MLA_ROPE
    for hd in range(MLA_HEADS):
        blk = qpe[:, (hd // per_block) * LANES:(hd // per_block + 1) * LANES]
        sh = (hd % per_block) * MLA_ROPE
        if sh:
            blk = pltpu.roll(blk, LANES - sh, axis=1)
        blk = jnp.where(lane < MLA_ROPE, blk, 0.0)
        qcat_ref[:, hd * QCAT:hd * QCAT + LANES] = (zq[:, hd * LANES:(hd + 1) * LANES] * qscale).astype(BF16)
        qcat_ref[:, hd * QCAT + LANES:(hd + 1) * QCAT] = blk.astype(BF16)

    ckv = _rms(seg(C_KV, C_KPE), gkv_ref[...], NORM_EPS)
    kpe = seg(C_KPE, C_KPESW) * ck_ref[...] + seg(C_KPESW, C_DQ) * sk_ref[...]
    ckv_ref[...] = ckv
    kpet_ref[0] = kpe.T[:MLA_ROPE]
    kvcat_ref[:, :LANES] = ckv.astype(BF16)
    kvcat_ref[:, LANES:] = kpe.astype(BF16)
    ones_rows = jnp.ones((VT_ROWS - MLA_KVR, tm), BF16)
    ckvt_ref[0, 0:MLA_KVR, :] = ckv.T.astype(BF16)
    ckvt_ref[0, MLA_KVR:VT_ROWS, :] = ones_rows

    dq_ref[...] = (seg(C_DQ, C_DK) * LOG2E).astype(BF16)
    dk = seg(C_DK, C_DV)
    dkt_ref[0] = dk.T
    dk16_ref[...] = dk.astype(BF16)
    dv = seg(C_DV, C_GA)
    for hd in range(DIFF_HEADS):
        dv_ref[pl.ds(hd, tm, stride=DIFF_HEADS), :] = dv[:, hd * DIFF_V:(hd + 1) * DIFF_V]
    for hd in range(DIFF_HEADS):
        dvt_ref[0, hd, 0:DIFF_V, :] = dv[:, hd * DIFF_V:(hd + 1) * DIFF_V].T.astype(BF16)
        dvt_ref[0, hd, DIFF_V:VT_ROWS, :] = ones_rows

    ga_ref[...] = jax.nn.sigmoid(seg(C_GA, C_GB))
    gb_ref[...] = jax.nn.sigmoid(seg(C_GB, C_END))


def _proj(x, tables, tm, weights):
    g_attn, w_in_r, g_q, w_q2, g_kv = weights
    cq, sq, ck, sk = tables
    n = x.shape[0]
    nb = n // tm
    seq = cq.shape[0]
    tb = seq // tm
    batch = n // seq
    tcol = lambda i: (i // tb, 0, i % tb)
    row = lambda i: (i, 0)
    trow = lambda i: (i % tb, 0)
    in_specs = [
        pl.BlockSpec((tm, D_MODEL), row),
        _const_spec((1, D_MODEL)),
        _const_spec((D_MODEL, C_END)),
        _const_spec((1, MLA_QR)),
        _const_spec((MLA_QR, w_q2.shape[1])),
        _const_spec((1, MLA_KVR)),
        pl.BlockSpec((tm, 2 * LANES), trow), pl.BlockSpec((tm, 2 * LANES), trow),
        pl.BlockSpec((tm, LANES), trow), pl.BlockSpec((tm, LANES), trow),
    ]
    out_shape = (
        jax.ShapeDtypeStruct((n, MLA_KVR), F32),
        jax.ShapeDtypeStruct((batch, MLA_ROPE, seq), F32),
        jax.ShapeDtypeStruct((batch, 2 * DIFF_HEADS * DIFF_QK, seq), F32),
        jax.ShapeDtypeStruct((n * DIFF_HEADS, DIFF_V), F32),
        jax.ShapeDtypeStruct((n, QCAT), BF16),
        jax.ShapeDtypeStruct((nb, VT_ROWS, tm), BF16),
        jax.ShapeDtypeStruct((n, MLA_HEADS * QCAT), BF16),
        jax.ShapeDtypeStruct((n, 2 * DIFF_HEADS * DIFF_QK), BF16),
        jax.ShapeDtypeStruct((n, 2 * DIFF_HEADS * DIFF_QK), BF16),
        jax.ShapeDtypeStruct((nb, DIFF_HEADS, VT_ROWS, tm), BF16),
        jax.ShapeDtypeStruct((n, D_MODEL), F32),
        jax.ShapeDtypeStruct((n, D_MODEL), F32),
    )
    out_specs = (
        pl.BlockSpec((tm, MLA_KVR), row),
        pl.BlockSpec((1, MLA_ROPE, tm), tcol),
        pl.BlockSpec((1, 512, tm), tcol),
        pl.BlockSpec((tm * DIFF_HEADS, DIFF_V), row),
        pl.BlockSpec((tm, QCAT), row),
        pl.BlockSpec((1, VT_ROWS, tm), lambda i: (i, 0, 0)),
        pl.BlockSpec((tm, MLA_HEADS * QCAT), row),
        pl.BlockSpec((tm, 512), row),
        pl.BlockSpec((tm, 512), row),
        pl.BlockSpec((1, DIFF_HEADS, VT_ROWS, tm), lambda i: (i, 0, 0, 0)),
        pl.BlockSpec((tm, D_MODEL), row),
        pl.BlockSpec((tm, D_MODEL), row),
    )
    return pl.pallas_call(
        _proj_kernel,
        grid=(nb,),
        in_specs=in_specs,
        out_specs=out_specs,
        out_shape=out_shape,
        compiler_params=_cparams(("parallel",)),
        name="proj_in",
    )(x, g_attn, w_in_r, g_q, w_q2, g_kv, cq, sq, ck, sk)


def _lambda(lq1_ref, lk1_ref, lq2_ref, lk2_ref):
    a = jnp.sum(lq1_ref[...] * lk1_ref[...], axis=-1, keepdims=True)
    b = jnp.sum(lq2_ref[...] * lk2_ref[...], axis=-1, keepdims=True)
    return jnp.exp(a) - jnp.exp(b) + LAM_INIT


NQ_MLA = MLA_HEADS * BQ
NQ_DIFF = 2 * BQ
NQ_ALL = NQ_MLA + DIFF_HEADS * NQ_DIFF
GROUP_KEYS = DEC_PAGES * PAGE
N_SLOTS = 2


def _fused_kernel(pt_ref, sch_ref,
                  qcat_ref, dq_ref, kv_ref, ckvt_ref, dk_ref, dvt_ref, bias_ref,
                  lq1_ref, lk1_ref, lq2_ref, lk2_ref, subg_ref, subgr_ref,
                  sq_ref, sdq_ref, skv_ref, sdk_ref, sdv_ref, bd_ref, bs_ref,
                  cckv_hbm, ckpe_hbm, cdk_hbm, cdv_hbm,
                  olat_ref, odiff_ref, solat_ref, sodiff_ref,
                  qall_scr, qd_scr, m_scr, acc_scr,
                  ckv_buf, kpe_buf, dkt_buf, dv_buf, sem,
                  smm_scr, slm_scr, sam_scr, smd_scr, sld_scr, sad_scr,
                  *, n_pages, groups_per_batch):
    b = pl.program_id(0)
    i = pl.program_id(1)
    nq = pl.num_programs(1)
    step = b * nq + i
    n_steps = pl.num_programs(0) * nq
    groups_per_row = n_pages // DEC_PAGES
    n_total = pl.num_programs(0) * groups_per_batch
    g_base = b * groups_per_batch + sch_ref[i]
    n_dec = sch_ref[nq + i]
    nrow = 2 * DIFF_HEADS

    def group_copies(row, grp, slot):
        cps = []
        for k in range(DEC_PAGES):
            pg = pt_ref[row * n_pages + grp * DEC_PAGES + k]
            keys = pl.ds(k * PAGE, PAGE)
            cps.append(pltpu.make_async_copy(cckv_hbm.at[pg], ckv_buf.at[slot, keys, :], sem.at[slot, 0]))
            cps.append(pltpu.make_async_copy(ckpe_hbm.at[pg], kpe_buf.at[slot, :, keys], sem.at[slot, 1]))
            cps.append(pltpu.make_async_copy(cdk_hbm.at[pg], dkt_buf.at[slot, :, keys], sem.at[slot, 2]))
            cps.append(pltpu.make_async_copy(cdv_hbm.at[pg],
                                             dv_buf.at[slot, pl.ds(k * PAGE * DIFF_HEADS, PAGE * DIFF_HEADS), :],
                                             sem.at[slot, 3]))
        return cps

    def start_group(row, grp, slot):
        for c in group_copies(row, grp, slot):
            c.start()

    def wait_group(row, grp, slot):
        for c in group_copies(row, grp, slot):
            c.wait()

    @pl.when(step == 0)
    def _():
        start_group(0, 0, 0)

    def decode_group(g):
        gl = g_base + g
        row = gl // groups_per_row
        grp = gl % groups_per_row
        slot = gl % N_SLOTS
        wait_group(row, grp, slot)

        gn = jnp.minimum(gl + 1, n_total - 1)
        start_group(gn // groups_per_row, gn % groups_per_row, (gl + 1) % N_SLOTS)

        q = sq_ref[pl.ds(row * MLA_HEADS, MLA_HEADS), :].astype(BF16)
        qa = q[:, :MLA_KVR]
        qp = q[:, MLA_KVR:MLA_KVR + MLA_ROPE]
        rowi = lax.broadcasted_iota(jnp.int32, (nrow, 512), 0)
        lanei = lax.broadcasted_iota(jnp.int32, (nrow, 512), 1)
        dq_b = jnp.broadcast_to(sdq_ref[pl.ds(row, 1), :], (nrow, 512))
        qbd = jnp.where((lanei >> 6) == rowi, dq_b, 0.0).astype(BF16)

        first = grp == 0
        kvs = skv_ref[pl.ds(row, 1), :]
        smm_scr[...] = jnp.where(first, jnp.sum(q.astype(F32) * kvs, axis=1, keepdims=True), smm_scr[...])
        slm_scr[...] = jnp.where(first, 1.0, slm_scr[...])
        sam_scr[...] = jnp.where(first, jnp.broadcast_to(kvs[:, :MLA_KVR], sam_scr.shape), sam_scr[...])
        s0 = jnp.sum(qbd.astype(F32) * sdk_ref[pl.ds(row, 1), :], axis=1, keepdims=True)
        smd_scr[...] = jnp.where(first, s0 + bs_ref[:, 0:1], smd_scr[...])
        sld_scr[...] = jnp.where(first, 1.0, sld_scr[...])
        sad_scr[...] = jnp.where(first, jnp.broadcast_to(sdv_ref[pl.ds(row, 1), :], sad_scr.shape), sad_scr[...])

        tdims = (((1,), (1,)), ((), ()))
        ckv16 = ckv_buf[slot].astype(BF16)
        sm = (lax.dot_general(qa, ckv16, tdims, preferred_element_type=F32)
              + jnp.dot(qp, kpe_buf[slot].astype(BF16), preferred_element_type=F32))
        sd = jnp.dot(qbd, dkt_buf[slot].astype(BF16), preferred_element_type=F32) + bd_ref[grp]

        def merge(s, m_scr, l_scr):
            m_old = m_scr[...]
            m_new = jnp.maximum(m_old, jnp.max(s, axis=1, keepdims=True))
            a = jnp.exp2(m_old - m_new)
            p = jnp.exp2(s - m_new)
            l_scr[...] = l_scr[...] * a + jnp.sum(p, axis=1, keepdims=True)
            m_scr[...] = m_new
            return a, p.astype(BF16)

        a, p = merge(sm, smm_scr, slm_scr)
        sam_scr[...] = sam_scr[...] * a + jnp.dot(p, ckv16, preferred_element_type=F32)
        a, p = merge(sd, smd_scr, sld_scr)
        v_all = jnp.concatenate(
            [dv_buf[slot, pl.ds(hd, GROUP_KEYS, stride=DIFF_HEADS), :].astype(BF16) for hd in range(DIFF_HEADS)],
            axis=1)
        sad_scr[...] = sad_scr[...] * a + jnp.dot(p, v_all, preferred_element_type=F32)

        solat_ref[pl.ds(row * MLA_HEADS, MLA_HEADS), :] = sam_scr[...] / slm_scr[...]
        lam = _lambda(lq1_ref, lk1_ref, lq2_ref, lk2_ref)
        o = sad_scr[...] / sld_scr[...]
        for hd in range(DIFF_HEADS):
            o1 = o[2 * hd:2 * hd + 1, hd * DIFF_V:(hd + 1) * DIFF_V]
            o2 = o[2 * hd + 1:2 * hd + 2, hd * DIFF_V:(hd + 1) * DIFF_V]
            y = _rms(o1 - lam * o2, subgr_ref[...], SUBLN_EPS) * (1.0 - LAM_INIT)
            sodiff_ref[pl.ds(row * DIFF_HEADS + hd, 1), :] = y

    for hd in range(MLA_HEADS):
        qall_scr[hd * BQ:(hd + 1) * BQ, :] = qcat_ref[:, hd * QCAT:(hd + 1) * QCAT]
    lane = lax.broadcasted_iota(jnp.int32, (BQ, LANES), 1)
    for hd in range(DIFF_HEADS):
        dqh = dq_ref[:, hd * LANES:(hd + 1) * LANES]
        qd_scr[hd, 0:BQ, :] = jnp.where(lane < DIFF_QK, dqh, jnp.zeros_like(dqh))
        qd_scr[hd, BQ:2 * BQ, :] = jnp.where(lane >= DIFF_QK, dqh, jnp.zeros_like(dqh))

    m_scr[...] = jnp.full(m_scr.shape, NEG, F32)
    acc_scr[...] = jnp.zeros(acc_scr.shape, F32)
    tdims = (((1,), (1,)), ((), ()))

    def block(j, diag):
        koff = pl.multiple_of(j * BK, BK)
        st = lax.dot_general(kv_ref[pl.ds(koff, BK), :], qall_scr[...], tdims, preferred_element_type=F32)
        if diag:
            mask = bias_ref[3 * DIFF_HEADS]
            st = st + jnp.concatenate([mask] * MLA_HEADS, axis=1)
        d = 0 if diag else jnp.minimum(i - j, 2)
        parts = [st]
        for hd in range(DIFF_HEADS):
            kb = dk_ref[pl.ds(koff, BK), hd * LANES:(hd + 1) * LANES]
            b1 = bias_ref[d * DIFF_HEADS + hd]
            sh = lax.dot_general(kb, qd_scr[hd], tdims, preferred_element_type=F32)
            parts.append(sh + jnp.concatenate([b1, b1], axis=1))
        st = jnp.concatenate(parts, axis=1)
        m_old = m_scr[...]
        m_new = jnp.maximum(m_old, jnp.max(st, axis=0, keepdims=True))
        a = jnp.exp2(m_old - m_new)
        pt = jnp.exp2(st - m_new).astype(BF16)
        m_scr[...] = m_new
        cs = slice(0, NQ_MLA)
        acc_scr[:, cs] = acc_scr[:, cs] * a[:, cs] + jnp.dot(ckvt_ref[j], pt[:, cs], preferred_element_type=F32)
        for hd in range(DIFF_HEADS):
            cs = slice(NQ_MLA + hd * NQ_DIFF, NQ_MLA + (hd + 1) * NQ_DIFF)
            acc_scr[:, cs] = acc_scr[:, cs] * a[:, cs] + jnp.dot(dvt_ref[j, hd], pt[:, cs],
                                                               preferred_element_type=F32)

    def paired(j, carry):
        decode_group(j)
        block(j, False)
        return carry

    def alone(j, carry):
        block(j, False)
        return carry

    def tail(j, carry):
        decode_group(j)
        return carry

    n_paired = jnp.minimum(i, n_dec)
    lax.fori_loop(0, n_paired, paired, 0)
    lax.fori_loop(n_paired, i, alone, 0)

    @pl.when(i < n_dec)
    def _():
        decode_group(i)

    block(i, True)
    lax.fori_loop(i + 1, n_dec, tail, 0)

    @pl.when(step == n_steps - 1)
    def _():
        last = n_total - 1
        wait_group(last // groups_per_row, last % groups_per_row, n_total % N_SLOTS)

    acc = acc_scr[...]
    o = acc[:MLA_KVR] / acc[MLA_KVR:MLA_KVR + 1]
    for hd in range(MLA_HEADS):
        olat_ref[:, hd * LANES:(hd + 1) * LANES] = o[:, hd * BQ:(hd + 1) * BQ].T.astype(BF16)
    lam = _lambda(lq1_ref, lk1_ref, lq2_ref, lk2_ref)
    for hd in range(DIFF_HEADS):
        c0 = NQ_MLA + hd * NQ_DIFF
        od = o[:, c0:c0 + BQ] - lam * o[:, c0 + BQ:c0 + 2 * BQ]
        y = od * lax.rsqrt(jnp.mean(od * od, axis=0, keepdims=True) + SUBLN_EPS) * subg_ref[...]
        odiff_ref[:, hd * LANES:(hd + 1) * LANES] = (y * (1.0 - LAM_INIT)).T.astype(BF16)


def _attn_fused(p_in, s_in, caches, page_table, bias, bias_dec, bias_self, lam_vecs, subg, batch, seq):
    qcat, dq, kvcat, ckvt, dk16, dvt = p_in
    s_qcat, s_dq, s_kvcat, s_dk16, s_dv16 = s_in
    nq = seq // BQ
    nk = seq // BK
    n = batch * seq
    nb, n_pages = page_table.shape
    assert n_pages % DEC_PAGES == 0 and (nb * (n_pages // DEC_PAGES)) % batch == 0
    gpb = nb * (n_pages // DEC_PAGES) // batch
    blocks = np.cumsum(np.arange(1, nq + 1))
    cum = np.round(gpb * blocks / blocks[-1]).astype(np.int64)
    cnt = np.diff(np.concatenate([[0], cum]))
    sched = jnp.asarray(np.concatenate([cum - cnt, cnt]), dtype=jnp.int32)
    pt = page_table.reshape(-1).astype(jnp.int32)

    qrow = lambda b, i, pt, sch: (b * nq + i, 0)

    def const(shape):
        nd = len(shape)
        return pl.BlockSpec(shape, lambda b, i, pt, sch: (0,) * nd, pipeline_mode=pl.Buffered(1))

    in_specs = [
        pl.BlockSpec((BQ, MLA_HEADS * QCAT), qrow),
        pl.BlockSpec((BQ, 512), qrow),
        pl.BlockSpec((seq, QCAT), lambda b, i, pt, sch: (b, 0), pipeline_mode=pl.Buffered(1)),
        pl.BlockSpec((nk, VT_ROWS, BK), lambda b, i, pt, sch: (b, 0, 0), pipeline_mode=pl.Buffered(1)),
        pl.BlockSpec((seq, 512), lambda b, i, pt, sch: (b, 0), pipeline_mode=pl.Buffered(1)),
        pl.BlockSpec((nk, DIFF_HEADS, VT_ROWS, BK), lambda b, i, pt, sch: (b, 0, 0, 0),
                     pipeline_mode=pl.Buffered(1)),
        const(bias.shape),
        const((1, DIFF_QK)), const((1, DIFF_QK)), const((1, DIFF_QK)), const((1, DIFF_QK)),
        const((DIFF_V, 1)), const((1, DIFF_V)),
        const((nb * MLA_HEADS, QCAT)), const((nb, 512)), const((nb, QCAT)), const((nb, 512)), const((nb, 512)),
        const(bias_dec.shape),
        const(bias_self.shape),
        pl.BlockSpec(memory_space=pl.ANY), pl.BlockSpec(memory_space=pl.ANY),
        pl.BlockSpec(memory_space=pl.ANY), pl.BlockSpec(memory_space=pl.ANY),
    ]
    out_specs = (pl.BlockSpec((BQ, MLA_HEADS * LANES), qrow), pl.BlockSpec((BQ, DIFF_HEADS * DIFF_V), qrow),
                 const((nb * MLA_HEADS, LANES)), const((nb * DIFF_HEADS, DIFF_V)))
    out_shape = (jax.ShapeDtypeStruct((n, MLA_HEADS * LANES), BF16),
                 jax.ShapeDtypeStruct((n, DIFF_HEADS * DIFF_V), BF16),
                 jax.ShapeDtypeStruct((nb * MLA_HEADS, LANES), F32),
                 jax.ShapeDtypeStruct((nb * DIFF_HEADS, DIFF_V), F32))
    nrow = 2 * DIFF_HEADS
    scratch = [
        pltpu.VMEM((NQ_MLA, QCAT), BF16),
        pltpu.VMEM((DIFF_HEADS, NQ_DIFF, LANES), BF16),
        pltpu.VMEM((1, NQ_ALL), F32), pltpu.VMEM((VT_ROWS, NQ_ALL), F32),
        pltpu.VMEM((N_SLOTS, GROUP_KEYS, MLA_KVR), F32),
        pltpu.VMEM((N_SLOTS, MLA_ROPE, GROUP_KEYS), F32),
        pltpu.VMEM((N_SLOTS, 512, GROUP_KEYS), F32),
        pltpu.VMEM((N_SLOTS, GROUP_KEYS * DIFF_HEADS, DIFF_V), F32),
        pltpu.SemaphoreType.DMA((N_SLOTS, 4)),
        pltpu.VMEM((MLA_HEADS, 1), F32), pltpu.VMEM((MLA_HEADS, 1), F32), pltpu.VMEM((MLA_HEADS, MLA_KVR), F32),
        pltpu.VMEM((nrow, 1), F32), pltpu.VMEM((nrow, 1), F32), pltpu.VMEM((nrow, 512), F32),
    ]
    grid_spec = pltpu.PrefetchScalarGridSpec(
        num_scalar_prefetch=2, grid=(batch, nq), in_specs=in_specs, out_specs=out_specs, scratch_shapes=scratch)
    olat, odiff, s_olat, s_odiff = pl.pallas_call(
        functools.partial(_fused_kernel, n_pages=n_pages, groups_per_batch=gpb),
        grid_spec=grid_spec,
        out_shape=out_shape,
        compiler_params=_cparams(("arbitrary", "arbitrary")),
        name="attn_fused",
    )(pt, sched, qcat, dq, kvcat, ckvt, dk16, dvt, bias, *lam_vecs, subg.reshape(-1, 1), subg.reshape(1, -1),
      s_qcat.reshape(nb * MLA_HEADS, QCAT).astype(F32), s_dq.astype(F32), s_kvcat.astype(F32),
      s_dk16.astype(F32), s_dv16.astype(F32), bias_dec, bias_self, *caches)
    return (olat, odiff, s_olat.reshape(nb, MLA_HEADS * LANES).astype(BF16),
            s_odiff.reshape(nb, DIFF_HEADS * DIFF_V).astype(BF16))


def _post_kernel(x_ref, olat_ref, odiff_ref, ga_ref, gb_ref, wf2_ref, wbd_ref, wo_ref,
                 gffn_ref, wg_ref, wu_ref, wd_ref, gfin_ref, y_ref):
    bm = jnp.dot(olat_ref[...], wf2_ref[...], preferred_element_type=F32)
    bd = jnp.dot(odiff_ref[...], wbd_ref[...], preferred_element_type=F32)
    merged = (ga_ref[...] * bm + gb_ref[...] * bd).astype(BF16)
    x = x_ref[...] + jnp.dot(merged, wo_ref[...], preferred_element_type=F32)
    h2 = _rms(x, gffn_ref[...], NORM_EPS).astype(BF16)
    gate = jnp.dot(h2, wg_ref[...], preferred_element_type=F32)
    up = jnp.dot(h2, wu_ref[...], preferred_element_type=F32)
    act = (gate * jax.nn.sigmoid(gate) * up).astype(BF16)
    x = x + jnp.dot(act, wd_ref[...], preferred_element_type=F32)
    y_ref[...] = _rms(x, gfin_ref[...], NORM_EPS)


def _post(x, olat, odiff, ga, gb, weights, tm):
    n = x.shape[0]
    row = lambda i: (i, 0)
    wf2, wbd, wo, gffn, wg, wu, wd, gfin = weights
    single = pl.Buffered(1)

    def wspec(shape):
        return pl.BlockSpec(shape, lambda i: (0, 0), pipeline_mode=single)

    in_specs = [
        pl.BlockSpec((tm, D_MODEL), row),
        pl.BlockSpec((tm, MLA_HEADS * LANES), row),
        pl.BlockSpec((tm, 512), row),
        pl.BlockSpec((tm, D_MODEL), row),
        pl.BlockSpec((tm, D_MODEL), row),
        wspec(wf2.shape), wspec(wbd.shape), wspec(wo.shape), wspec(gffn.shape),
        wspec(wg.shape), wspec(wu.shape), wspec(wd.shape), wspec(gfin.shape),
    ]
    return pl.pallas_call(
        _post_kernel,
        grid=(n // tm,),
        in_specs=in_specs,
        out_specs=pl.BlockSpec((tm, D_MODEL), row),
        out_shape=jax.ShapeDtypeStruct((n, D_MODEL), F32),
        compiler_params=_cparams(("parallel",)),
        name="merge_ffn",
    )(x, olat, odiff, ga, gb, wf2, wbd, wo, gffn, wg, wu, wd, gfin)


def kernel(x_prompt, x_sample, cache_mla_ckv, cache_mla_kpe, cache_diff_k, cache_diff_v, page_table, rel_bias_table, norm_attn_g, w_in, mla_q_norm_g, mla_w_uq, mla_kv_norm_g, mla_w_uk, mla_w_uv, diff_lambda_q1, diff_lambda_k1, diff_lambda_q2, diff_lambda_k2, diff_subln_g, w_branch_mla, w_branch_diff, w_o, norm_ffn_g, w_ffn_gate, w_ffn_up, w_ffn_down, norm_final_g):
    batch, seq, _ = x_prompt.shape
    nb, dec_seq, _ = x_sample.shape
    assert dec_seq == 1 and w_in.shape[0] == 1
    n_pages = page_table.shape[1]
    past_len = n_pages * PAGE
    assert REL_MAX_DIST <= BQ + 1 and seq % BQ == 0
    n_pool = cache_mla_ckv.shape[1]

    w = w_in[0]
    sw = np.concatenate([np.arange(16, 32), np.arange(0, 16)])
    z96 = jnp.zeros((D_MODEL, LANES - MLA_ROPE), F32)
    w_kpe = w[:, 512:544]
    w_in_r = jnp.concatenate(
        [w[:, 0:384], w[:, 384:512], w_kpe, z96, w_kpe[:, sw], z96,
         w[:, 544:1056] * DIFF_SCALE, w[:, 1056:1568], w[:, 1568:2080], w[:, 2080:3104], w[:, 3104:4128]],
        axis=1).astype(BF16)
    w_fold, w_f2 = _fold_weights(mla_w_uq[0], mla_w_uk[0], mla_w_uv[0], w_branch_mla[0])
    uq = mla_w_uq[0].reshape(MLA_QR, MLA_HEADS, MLA_NOPE + MLA_ROPE)
    w_pe = uq[:, :, MLA_NOPE:]
    w_q2 = jnp.concatenate([w_fold, w_pe.reshape(MLA_QR, -1), w_pe[:, :, sw].reshape(MLA_QR, -1)],
                           axis=1).astype(BF16)
    proj_w = (norm_attn_g[0].reshape(1, -1), w_in_r, mla_q_norm_g[0].reshape(1, -1), w_q2,
              mla_kv_norm_g[0].reshape(1, -1))
    post_w = (w_f2.astype(BF16), w_branch_diff[0].astype(BF16), w_o[0].astype(BF16),
              norm_ffn_g[0].reshape(1, -1), w_ffn_gate[0].astype(BF16), w_ffn_up[0].astype(BF16),
              w_ffn_down[0].astype(BF16), norm_final_g.reshape(1, -1))
    lam_vecs = tuple(v[0].reshape(1, -1) for v in (diff_lambda_q1, diff_lambda_k1, diff_lambda_q2, diff_lambda_k2))
    subg = diff_subln_g[0]

    tab_p = _rope_tables(jnp.arange(seq, dtype=jnp.int32))
    tab_s = _rope_tables(jnp.full((nb,), past_len, dtype=jnp.int32))
    bias_p, bias_dec, bias_self = _bias_tables(rel_bias_table, past_len)

    xp = x_prompt.reshape(batch * seq, D_MODEL)
    (p_ckv, p_kpe, p_dk, p_dv, p_kvcat, p_ckvt, p_qcat, p_dq, p_dk16, p_dvt, p_ga, p_gb) = _proj(
        xp, tab_p, BK, proj_w)
    xs = x_sample.reshape(nb, D_MODEL)
    (s_ckv, s_kpe, s_dk, s_dv, s_kvcat, _, s_qcat, s_dq, s_dk16, s_dvt, s_ga, s_gb) = _proj(
        xs, tab_s, nb, proj_w)

    caches = (cache_mla_ckv.reshape(n_pool, PAGE, MLA_KVR),
              jnp.transpose(cache_mla_kpe, (0, 1, 3, 2)).reshape(n_pool, MLA_ROPE, PAGE),
              jnp.transpose(cache_diff_k, (0, 1, 3, 4, 5, 2)).reshape(n_pool, 512, PAGE),
              cache_diff_v.reshape(n_pool, PAGE * DIFF_HEADS, DIFF_V))
    p_olat, p_odiff, s_olat, s_odiff = _attn_fused(
        (p_qcat, p_dq, p_kvcat, p_ckvt, p_dk16, p_dvt),
        (s_qcat, s_dq, s_kvcat, s_dk16, s_dv.reshape(nb, DIFF_HEADS * DIFF_V).astype(BF16)),
        caches, page_table, bias_p, bias_dec, bias_self, lam_vecs, subg, batch, seq)

    y_prompt = _post(xp, p_olat, p_odiff, p_ga, p_gb, post_w, 256).reshape(batch, seq, D_MODEL)
    y_sample = _post(xs, s_olat, s_odiff, s_ga, s_gb, post_w, nb).reshape(nb, 1, D_MODEL)

    def kpe_out(t, b, s):
        return jnp.transpose(t, (0, 2, 1)).reshape(1, b, s, MLA_ROPE)

    def dk_out(t, b, s):
        t = t.reshape(b, DIFF_HEADS, 2, DIFF_QK, s)
        return jnp.transpose(t, (0, 4, 1, 2, 3)).reshape(1, b, s, DIFF_HEADS, 2, DIFF_QK)

    return (y_prompt, y_sample,
            p_ckv.reshape(1, batch, seq, MLA_KVR), kpe_out(p_kpe, batch, seq),
            dk_out(p_dk, batch, seq), p_dv.reshape(1, batch, seq, DIFF_HEADS, DIFF_V),
            s_ckv.reshape(1, nb, 1, MLA_KVR), kpe_out(s_kpe, 1, nb).reshape(1, nb, 1, MLA_ROPE),
            dk_out(s_dk, 1, nb).reshape(1, nb, 1, DIFF_HEADS, 2, DIFF_QK),
            s_dv.reshape(1, nb, 1, DIFF_HEADS, DIFF_V))
```

```python
import functools
import math

import numpy as np
import jax
import jax.numpy as jnp
from jax import lax
from jax.experimental import pallas as pl
from jax.experimental.pallas import tpu as pltpu

F32 = jnp.float32
BF16 = jnp.bfloat16

D_MODEL = 1024
PAGE = 128
MLA_HEADS = 8
MLA_NOPE = 64
MLA_ROPE = 32
MLA_V = 64
MLA_QR = 384
MLA_KVR = 128
MLA_SCALE = (MLA_NOPE + MLA_ROPE) ** -0.5
ROPE_THETA = 10000.0
DIFF_HEADS = 4
DIFF_QK = 64
DIFF_V = 128
DIFF_SCALE = DIFF_QK ** -0.5
SUBLN_EPS = 1e-5
REL_BUCKETS = 32
REL_MAX_DIST = 128
FFN_DIM = 2816
NORM_EPS = 1e-6
LAM_INIT = 0.8 - 0.6 * math.exp(-0.3 * 0)

LANES = 128
QCAT = 2 * LANES
NEG = -1e30
LOG2E = math.log2(math.e)
VMEM_LIMIT = 56 * 1024 * 1024

C_QLAT = 0
C_KV = 384
C_KPE = 512
C_KPESW = 640
C_DQ = 768
C_DK = 1280
C_DV = 1792
C_GA = 2304
C_GB = 3328
C_END = 4352

BQ = 256
BK = 256
DEC_PAGES = 16
VT_ROWS = MLA_KVR + 16


def _cparams(sem):
    return pltpu.CompilerParams(dimension_semantics=sem, vmem_limit_bytes=VMEM_LIMIT)


def _const_spec(shape):
    nd = len(shape)
    return pl.BlockSpec(shape, lambda *_: (0,) * nd, pipeline_mode=pl.Buffered(1))


def _rope_tables_kernel(pos_ref, inv_ref, cq_ref, sq_ref, ck_ref, sk_ref):
    ang = pos_ref[...] * inv_ref[...]
    c = jnp.cos(ang)
    s = jnp.sin(ang)
    lane = lax.broadcasted_iota(jnp.int32, ang.shape, 1)
    first_half = (lane & (MLA_ROPE - 1)) < (MLA_ROPE // 2)
    s_signed = jnp.where(first_half, -s, s)
    cq_ref[...] = c
    sq_ref[...] = s_signed
    kmask = lax.broadcasted_iota(jnp.int32, ck_ref.shape, 1) < MLA_ROPE
    ck_ref[...] = jnp.where(kmask, c[:, :LANES], 0.0)
    sk_ref[...] = jnp.where(kmask, s_signed[:, :LANES], 0.0)


def _rope_tables(pos):
    n = pos.shape[0]
    half = MLA_ROPE // 2
    inv = (ROPE_THETA ** (-jnp.arange(half, dtype=F32) / half))
    inv = jnp.tile(inv, 2 * MLA_HEADS).reshape(1, 2 * LANES)
    return pl.pallas_call(
        _rope_tables_kernel,
        out_shape=(jax.ShapeDtypeStruct((n, 2 * LANES), F32), jax.ShapeDtypeStruct((n, 2 * LANES), F32),
                   jax.ShapeDtypeStruct((n, LANES), F32), jax.ShapeDtypeStruct((n, LANES), F32)),
        name="rope_tables",
    )(pos.astype(F32).reshape(n, 1), inv)


def _bucket(n):
    max_exact = REL_BUCKETS // 2
    nf = jnp.maximum(n, 1).astype(F32)
    large = max_exact + (jnp.log(nf / max_exact) / math.log(REL_MAX_DIST / max_exact)
                         * (REL_BUCKETS - max_exact)).astype(jnp.int32)
    large = jnp.minimum(large, REL_BUCKETS - 1)
    return jnp.where(n < max_exact, n, large)


def _bias_kernel(tab_ref, bp_ref, bd_ref, bs_ref, *, past_len):
    kidx = lax.broadcasted_iota(jnp.int32, (BK, BQ), 0)
    qidx = lax.broadcasted_iota(jnp.int32, (BK, BQ), 1)
    for d in range(3):
        rel = d * BQ + qidx - kidx
        b = _bucket(jnp.maximum(rel, 0))
        for h in range(DIFF_HEADS):
            val = jnp.zeros((BK, BQ), F32)
            for t in range(REL_BUCKETS):
                val = jnp.where(b == t, tab_ref[t, h] * LOG2E, val)
            if d == 0:
                val = jnp.where(rel >= 0, val, NEG)
            bp_ref[d * DIFF_HEADS + h] = val
    bp_ref[3 * DIFF_HEADS] = jnp.where(qidx - kidx >= 0, 0.0, NEG)
    gk = DEC_PAGES * PAGE
    row = lax.broadcasted_iota(jnp.int32, (2 * DIFF_HEADS, gk), 0)
    for g in range(past_len // gk):
        kpos = g * gk + lax.broadcasted_iota(jnp.int32, (2 * DIFF_HEADS, gk), 1)
        b = _bucket(jnp.maximum(past_len - kpos, 0))
        val = jnp.zeros((2 * DIFF_HEADS, gk), F32)
        for h in range(DIFF_HEADS):
            for t in range(REL_BUCKETS):
                val = jnp.where((b == t) & ((row >> 1) == h), tab_ref[t, h] * LOG2E, val)
        bd_ref[g] = val
    rows = lax.broadcasted_iota(jnp.int32, (2 * DIFF_HEADS, LANES), 0)
    b0 = _bucket(jnp.zeros((2 * DIFF_HEADS, LANES), jnp.int32))
    vs = jnp.zeros((2 * DIFF_HEADS, LANES), F32)
    for h in range(DIFF_HEADS):
        for t in range(REL_BUCKETS):
            vs = jnp.where((b0 == t) & ((rows >> 1) == h), tab_ref[t, h] * LOG2E, vs)
    bs_ref[...] = vs


def _bias_tables(table, past_len):
    return pl.pallas_call(
        functools.partial(_bias_kernel, past_len=past_len),
        in_specs=[pl.BlockSpec(memory_space=pltpu.SMEM)],
        out_shape=(jax.ShapeDtypeStruct((3 * DIFF_HEADS + 1, BK, BQ), F32),
                   jax.ShapeDtypeStruct((past_len // (DEC_PAGES * PAGE), 2 * DIFF_HEADS, DEC_PAGES * PAGE), F32),
                   jax.ShapeDtypeStruct((2 * DIFF_HEADS, LANES), F32)),
        compiler_params=pltpu.CompilerParams(vmem_limit_bytes=VMEM_LIMIT),
        name="bias_tables",
    )(table.astype(F32))


def _fold_kernel(nope_ref, uk_ref, uv_ref, bm_ref, wfold_ref, wf2_ref):
    for h in range(MLA_HEADS):
        wfold_ref[:, h * LANES:(h + 1) * LANES] = lax.dot_general(
            nope_ref[h], uk_ref[h], (((1,), (1,)), ((), ())),
            preferred_element_type=F32, precision=lax.Precision.HIGHEST)
        wf2_ref[h * LANES:(h + 1) * LANES, :] = jnp.dot(
            uv_ref[h], bm_ref[h], preferred_element_type=F32, precision=lax.Precision.HIGHEST)


def _fold_weights(w_uq, w_uk, w_uv, w_bm):
    uq = w_uq.reshape(MLA_QR, MLA_HEADS, MLA_NOPE + MLA_ROPE)
    nope = jnp.transpose(uq[:, :, :MLA_NOPE], (1, 0, 2))
    uk = jnp.transpose(w_uk, (1, 0, 2))
    uv = jnp.transpose(w_uv, (1, 0, 2))
    bm = w_bm.reshape(MLA_HEADS, MLA_V, D_MODEL)
    return pl.pallas_call(
        _fold_kernel,
        out_shape=(jax.ShapeDtypeStruct((MLA_QR, MLA_HEADS * LANES), F32),
                   jax.ShapeDtypeStruct((MLA_HEADS * LANES, D_MODEL), F32)),
        compiler_params=pltpu.CompilerParams(vmem_limit_bytes=VMEM_LIMIT),
        name="fold_weights",
    )(nope, uk, uv, bm)


def _rms(x, g, eps):
    return x * lax.rsqrt(jnp.mean(x * x, axis=-1, keepdims=True) + eps) * g


def _proj_kernel(x_ref, g_ref, win_ref, gq_ref, wq2_ref, gkv_ref, cq_ref, sq_ref, ck_ref, sk_ref,
                 ckv_ref, kpet_ref, dkt_ref, dv_ref, kvcat_ref, ckvt_ref, qcat_ref, dq_ref, dk16_ref, dvt_ref,
                 ga_ref, gb_ref):
    tm = x_ref.shape[0]
    h = _rms(x_ref[...], g_ref[...], NORM_EPS).astype(BF16)

    def seg(lo, hi):
        return jnp.dot(h, win_ref[:, lo:hi], preferred_element_type=F32)

    z0 = seg(C_QLAT, C_KPE)
    qn = _rms(z0[:, :C_KV], gq_ref[...], NORM_EPS).astype(BF16)
    zq = jnp.dot(qn, wq2_ref[...], preferred_element_type=F32)
    nq = MLA_HEADS * LANES
    qscale = MLA_SCALE * LOG2E
    qpe = (zq[:, nq:nq + 2 * LANES] * cq_ref[...] + zq[:, nq + 2 * LANES:] * sq_ref[...]) * qscale
    lane = lax.broadcasted_iota(jnp.int32, (tm, LANES), 1)
    per_block = LANES // MLA_ROPE
    for hd in range(MLA_HEADS):
        blk = qpe[:, (hd // per_block) * LANES:(hd // per_block + 1) * LANES]
        sh = (hd % per_block) * MLA_ROPE
        if sh:
            blk = pltpu.roll(blk, LANES - sh, axis=1)
        blk = jnp.where(lane < MLA_ROPE, blk, 0.0)
        qcat_ref[:, hd * QCAT:hd * QCAT + LANES] = (zq[:, hd * LANES:(hd + 1) * LANES] * qscale).astype(BF16)
        qcat_ref[:, hd * QCAT + LANES:(hd + 1) * QCAT] = blk.astype(BF16)

    ckv = _rms(z0[:, C_KV:C_KPE], gkv_ref[...], NORM_EPS)
    zk = seg(C_KPE, C_DQ)
    kpe = zk[:, :LANES] * ck_ref[...] + zk[:, LANES:] * sk_ref[...]
    ckv_ref[...] = ckv
    kpet_ref[0] = kpe.T[:MLA_ROPE]
    kvcat_ref[:, :LANES] = ckv.astype(BF16)
    kvcat_ref[:, LANES:] = kpe.astype(BF16)
    ones_rows = jnp.ones((VT_ROWS - MLA_KVR, tm), BF16)
    ckvt_ref[0, 0:MLA_KVR, :] = ckv.T.astype(BF16)
    ckvt_ref[0, MLA_KVR:VT_ROWS, :] = ones_rows

    dq_ref[...] = (seg(C_DQ, C_DK) * LOG2E).astype(BF16)
    dk = seg(C_DK, C_DV)
    dkt_ref[0] = dk.T
    dk16_ref[...] = dk.astype(BF16)
    dv = seg(C_DV, C_GA)
    for hd in range(DIFF_HEADS):
        dv_ref[pl.ds(hd, tm, stride=DIFF_HEADS), :] = dv[:, hd * DIFF_V:(hd + 1) * DIFF_V]
    for hd in range(DIFF_HEADS):
        dvt_ref[0, hd, 0:DIFF_V, :] = dv[:, hd * DIFF_V:(hd + 1) * DIFF_V].T.astype(BF16)
        dvt_ref[0, hd, DIFF_V:VT_ROWS, :] = ones_rows

    ga_ref[...] = jax.nn.sigmoid(seg(C_GA, C_GB))
    gb_ref[...] = jax.nn.sigmoid(seg(C_GB, C_END))


def _proj(x, tables, tm, weights):
    g_attn, w_in_r, g_q, w_q2, g_kv = weights
    cq, sq, ck, sk = tables
    n = x.shape[0]
    nb = n // tm
    seq = cq.shape[0]
    tb = seq // tm
    batch = n // seq
    tcol = lambda i: (i // tb, 0, i % tb)
    row = lambda i: (i, 0)
    trow = lambda i: (i % tb, 0)
    in_specs = [
        pl.BlockSpec((tm, D_MODEL), row),
        _const_spec((1, D_MODEL)),
        _const_spec((D_MODEL, C_END)),
        _const_spec((1, MLA_QR)),
        _const_spec((MLA_QR, w_q2.shape[1])),
        _const_spec((1, MLA_KVR)),
        pl.BlockSpec((tm, 2 * LANES), trow), pl.BlockSpec((tm, 2 * LANES), trow),
        pl.BlockSpec((tm, LANES), trow), pl.BlockSpec((tm, LANES), trow),
    ]
    out_shape = (
        jax.ShapeDtypeStruct((n, MLA_KVR), F32),
        jax.ShapeDtypeStruct((batch, MLA_ROPE, seq), F32),
        jax.ShapeDtypeStruct((batch, 2 * DIFF_HEADS * DIFF_QK, seq), F32),
        jax.ShapeDtypeStruct((n * DIFF_HEADS, DIFF_V), F32),
        jax.ShapeDtypeStruct((n, QCAT), BF16),
        jax.ShapeDtypeStruct((nb, VT_ROWS, tm), BF16),
        jax.ShapeDtypeStruct((n, MLA_HEADS * QCAT), BF16),
        jax.ShapeDtypeStruct((n, 2 * DIFF_HEADS * DIFF_QK), BF16),
        jax.ShapeDtypeStruct((n, 2 * DIFF_HEADS * DIFF_QK), BF16),
        jax.ShapeDtypeStruct((nb, DIFF_HEADS, VT_ROWS, tm), BF16),
        jax.ShapeDtypeStruct((n, D_MODEL), F32),
        jax.ShapeDtypeStruct((n, D_MODEL), F32),
    )
    out_specs = (
        pl.BlockSpec((tm, MLA_KVR), row),
        pl.BlockSpec((1, MLA_ROPE, tm), tcol),
        pl.BlockSpec((1, 512, tm), tcol),
        pl.BlockSpec((tm * DIFF_HEADS, DIFF_V), row),
        pl.BlockSpec((tm, QCAT), row),
        pl.BlockSpec((1, VT_ROWS, tm), lambda i: (i, 0, 0)),
        pl.BlockSpec((tm, MLA_HEADS * QCAT), row),
        pl.BlockSpec((tm, 512), row),
        pl.BlockSpec((tm, 512), row),
        pl.BlockSpec((1, DIFF_HEADS, VT_ROWS, tm), lambda i: (i, 0, 0, 0)),
        pl.BlockSpec((tm, D_MODEL), row),
        pl.BlockSpec((tm, D_MODEL), row),
    )
    return pl.pallas_call(
        _proj_kernel,
        grid=(nb,),
        in_specs=in_specs,
        out_specs=out_specs,
        out_shape=out_shape,
        compiler_params=_cparams(("parallel",)),
        name="proj_in",
    )(x, g_attn, w_in_r, g_q, w_q2, g_kv, cq, sq, ck, sk)


def _lambda(lq1_ref, lk1_ref, lq2_ref, lk2_ref):
    a = jnp.sum(lq1_ref[...] * lk1_ref[...], axis=-1, keepdims=True)
    b = jnp.sum(lq2_ref[...] * lk2_ref[...], axis=-1, keepdims=True)
    return jnp.exp(a) - jnp.exp(b) + LAM_INIT


NQ_MLA = MLA_HEADS * BQ
NQ_DIFF = 2 * BQ
NQ_ALL = NQ_MLA + DIFF_HEADS * NQ_DIFF
GROUP_KEYS = DEC_PAGES * PAGE
N_SLOTS = 2


def _fused_kernel(pt_ref, sch_ref,
                  qcat_ref, dq_ref, kv_ref, ckvt_ref, dk_ref, dvt_ref, bias_ref,
                  lq1_ref, lk1_ref, lq2_ref, lk2_ref, subg_ref, subgr_ref,
                  sq_ref, sdq_ref, skv_ref, sdk_ref, sdv_ref, bd_ref, bs_ref,
                  cckv_hbm, ckpe_hbm, cdk_hbm, cdv_hbm,
                  olat_ref, odiff_ref, solat_ref, sodiff_ref,
                  qall_scr, qd_scr, m_scr, acc_scr,
                  ckv_buf, kpe_buf, dkt_buf, dv_buf, sem, ckv16_scr, kpe16_scr, dkt16_scr, v16_scr,
                  smm_scr, slm_scr, sam_scr, smd_scr, sld_scr, sad_scr,
                  *, n_pages, groups_per_batch):
    b = pl.program_id(0)
    i = pl.program_id(1)
    nq = pl.num_programs(1)
    step = b * nq + i
    n_steps = pl.num_programs(0) * nq
    groups_per_row = n_pages // DEC_PAGES
    n_total = pl.num_programs(0) * groups_per_batch
    g_base = b * groups_per_batch + sch_ref[i]
    n_dec = sch_ref[nq + i]
    nrow = 2 * DIFF_HEADS

    def group_copies(row, grp, slot):
        cps = []
        for k in range(DEC_PAGES):
            pg = pt_ref[row * n_pages + grp * DEC_PAGES + k]
            keys = pl.ds(k * PAGE, PAGE)
            cps.append(pltpu.make_async_copy(cckv_hbm.at[pg], ckv_buf.at[slot, keys, :], sem.at[slot, 0]))
            cps.append(pltpu.make_async_copy(ckpe_hbm.at[pg], kpe_buf.at[slot, :, keys], sem.at[slot, 1]))
            cps.append(pltpu.make_async_copy(cdk_hbm.at[pg], dkt_buf.at[slot, :, keys], sem.at[slot, 2]))
            cps.append(pltpu.make_async_copy(cdv_hbm.at[pg],
                                             dv_buf.at[slot, pl.ds(k * PAGE * DIFF_HEADS, PAGE * DIFF_HEADS), :],
                                             sem.at[slot, 3]))
        return cps

    def start_group(row, grp, slot):
        for c in group_copies(row, grp, slot):
            c.start()

    def wait_group(row, grp, slot):
        for c in group_copies(row, grp, slot):
            c.wait()

    @pl.when(step == 0)
    def _():
        start_group(0, 0, 0)
        start_group(1 // groups_per_row, 1 % groups_per_row, 1)

    def decode_group(g):
        gl = g_base + g
        row = gl // groups_per_row
        grp = gl % groups_per_row
        slot = gl % N_SLOTS
        wait_group(row, grp, slot)

        ckv16_scr[...] = ckv_buf[slot].astype(BF16)
        kpe16_scr[...] = kpe_buf[slot].astype(BF16)
        dkt16_scr[...] = dkt_buf[slot].astype(BF16)
        for hd in range(DIFF_HEADS):
            v16_scr[:, hd * DIFF_V:(hd + 1) * DIFF_V] = dv_buf[
                slot, pl.ds(hd, GROUP_KEYS, stride=DIFF_HEADS), :].astype(BF16)
        gn = jnp.minimum(gl + N_SLOTS, n_total - 1)
        start_group(gn // groups_per_row, gn % groups_per_row, slot)

        q = sq_ref[pl.ds(row * MLA_HEADS, MLA_HEADS), :].astype(BF16)
        qa = q[:, :MLA_KVR]
        qp = q[:, MLA_KVR:MLA_KVR + MLA_ROPE]
        rowi = lax.broadcasted_iota(jnp.int32, (nrow, 512), 0)
        lanei = lax.broadcasted_iota(jnp.int32, (nrow, 512), 1)
        dq_b = jnp.broadcast_to(sdq_ref[pl.ds(row, 1), :], (nrow, 512))
        qbd = jnp.where((lanei >> 6) == rowi, dq_b, 0.0).astype(BF16)

        first = grp == 0
        kvs = skv_ref[pl.ds(row, 1), :]
        smm_scr[...] = jnp.where(first, jnp.sum(q.astype(F32) * kvs, axis=1, keepdims=True), smm_scr[...])
        slm_scr[...] = jnp.where(first, 1.0, slm_scr[...])
        sam_scr[...] = jnp.where(first, jnp.broadcast_to(kvs[:, :MLA_KVR], sam_scr.shape), sam_scr[...])
        s0 = jnp.sum(qbd.astype(F32) * sdk_ref[pl.ds(row, 1), :], axis=1, keepdims=True)
        smd_scr[...] = jnp.where(first, s0 + bs_ref[:, 0:1], smd_scr[...])
        sld_scr[...] = jnp.where(first, 1.0, sld_scr[...])
        sad_scr[...] = jnp.where(first, jnp.broadcast_to(sdv_ref[pl.ds(row, 1), :], sad_scr.shape), sad_scr[...])

        tdims = (((1,), (1,)), ((), ()))
        ckv16 = ckv16_scr[...]
        sm = (lax.dot_general(qa, ckv16, tdims, preferred_element_type=F32)
              + jnp.dot(qp, kpe16_scr[...], preferred_element_type=F32))
        sd = jnp.dot(qbd, dkt16_scr[...], preferred_element_type=F32) + bd_ref[grp]

        def merge(s, m_scr, l_scr):
            m_old = m_scr[...]
            m_new = jnp.maximum(m_old, jnp.max(s, axis=1, keepdims=True))
            a = jnp.exp2(m_old - m_new)
            p = jnp.exp2(s - m_new)
            l_scr[...] = l_scr[...] * a + jnp.sum(p, axis=1, keepdims=True)
            m_scr[...] = m_new
            return a, p.astype(BF16)

        a, p = merge(sm, smm_scr, slm_scr)
        sam_scr[...] = sam_scr[...] * a + jnp.dot(p, ckv16, preferred_element_type=F32)
        a, p = merge(sd, smd_scr, sld_scr)
        sad_scr[...] = sad_scr[...] * a + jnp.dot(p, v16_scr[...], preferred_element_type=F32)

        solat_ref[pl.ds(row * MLA_HEADS, MLA_HEADS), :] = sam_scr[...] / slm_scr[...]
        lam = _lambda(lq1_ref, lk1_ref, lq2_ref, lk2_ref)
        o = sad_scr[...] / sld_scr[...]
        for hd in range(DIFF_HEADS):
            o1 = o[2 * hd:2 * hd + 1, hd * DIFF_V:(hd + 1) * DIFF_V]
            o2 = o[2 * hd + 1:2 * hd + 2, hd * DIFF_V:(hd + 1) * DIFF_V]
            y = _rms(o1 - lam * o2, subgr_ref[...], SUBLN_EPS) * (1.0 - LAM_INIT)
            sodiff_ref[pl.ds(row * DIFF_HEADS + hd, 1), :] = y

    for hd in range(MLA_HEADS):
        qall_scr[hd * BQ:(hd + 1) * BQ, :] = qcat_ref[:, hd * QCAT:(hd + 1) * QCAT]
    lane = lax.broadcasted_iota(jnp.int32, (BQ, LANES), 1)
    for hd in range(DIFF_HEADS):
        dqh = dq_ref[:, hd * LANES:(hd + 1) * LANES]
        qd_scr[hd, 0:BQ, :] = jnp.where(lane < DIFF_QK, dqh, jnp.zeros_like(dqh))
        qd_scr[hd, BQ:2 * BQ, :] = jnp.where(lane >= DIFF_QK, dqh, jnp.zeros_like(dqh))

    m_scr[...] = jnp.full(m_scr.shape, NEG, F32)
    acc_scr[...] = jnp.zeros(acc_scr.shape, F32)
    tdims = (((1,), (1,)), ((), ()))

    def block(j, diag):
        koff = pl.multiple_of(j * BK, BK)
        st = lax.dot_general(kv_ref[pl.ds(koff, BK), :], qall_scr[...], tdims, preferred_element_type=F32)
        if diag:
            mask = bias_ref[3 * DIFF_HEADS]
            st = st + jnp.concatenate([mask] * MLA_HEADS, axis=1)
        d = 0 if diag else jnp.minimum(i - j, 2)
        parts = [st]
        for hd in range(DIFF_HEADS):
            kb = dk_ref[pl.ds(koff, BK), hd * LANES:(hd + 1) * LANES]
            b1 = bias_ref[d * DIFF_HEADS + hd]
            sh = lax.dot_general(kb, qd_scr[hd], tdims, preferred_element_type=F32)
            parts.append(sh + jnp.concatenate([b1, b1], axis=1))
        st = jnp.concatenate(parts, axis=1)
        m_old = m_scr[...]
        m_new = jnp.maximum(m_old, jnp.max(st, axis=0, keepdims=True))
        a = jnp.exp2(m_old - m_new)
        pt = jnp.exp2(st - m_new).astype(BF16)
        m_scr[...] = m_new
        cs = slice(0, NQ_MLA)
        acc_scr[:, cs] = acc_scr[:, cs] * a[:, cs] + jnp.dot(ckvt_ref[j], pt[:, cs], preferred_element_type=F32)
        for hd in range(DIFF_HEADS):
            cs = slice(NQ_MLA + hd * NQ_DIFF, NQ_MLA + (hd + 1) * NQ_DIFF)
            acc_scr[:, cs] = acc_scr[:, cs] * a[:, cs] + jnp.dot(dvt_ref[j, hd], pt[:, cs],
                                                               preferred_element_type=F32)

    def paired(j, carry):
        decode_group(j)
        block(j, False)
        return carry

    def alone(j, carry):
        block(j, False)
        return carry

    def tail(j, carry):
        decode_group(j)
        return carry

    n_paired = jnp.minimum(i, n_dec)
    lax.fori_loop(0, n_paired, paired, 0)
    lax.fori_loop(n_paired, i, alone, 0)

    @pl.when(i < n_dec)
    def _():
        decode_group(i)

    block(i, True)
    lax.fori_loop(i + 1, n_dec, tail, 0)

    @pl.when(step == n_steps - 1)
    def _():
        last = n_total - 1
        for s_ in range(N_SLOTS):
            wait_group(last // groups_per_row, last % groups_per_row, s_)

    acc = acc_scr[...]
    o = acc[:MLA_KVR] / acc[MLA_KVR:MLA_KVR + 1]
    for hd in range(MLA_HEADS):
        olat_ref[:, hd * LANES:(hd + 1) * LANES] = o[:, hd * BQ:(hd + 1) * BQ].T.astype(BF16)
    lam = _lambda(lq1_ref, lk1_ref, lq2_ref, lk2_ref)
    for hd in range(DIFF_HEADS):
        c0 = NQ_MLA + hd * NQ_DIFF
        od = o[:, c0:c0 + BQ] - lam * o[:, c0 + BQ:c0 + 2 * BQ]
        y = od * lax.rsqrt(jnp.mean(od * od, axis=0, keepdims=True) + SUBLN_EPS) * subg_ref[...]
        odiff_ref[:, hd * LANES:(hd + 1) * LANES] = (y * (1.0 - LAM_INIT)).T.astype(BF16)


def _attn_fused(p_in, s_in, caches, page_table, bias, bias_dec, bias_self, lam_vecs, subg, batch, seq):
    qcat, dq, kvcat, ckvt, dk16, dvt = p_in
    s_qcat, s_dq, s_kvcat, s_dk16, s_dv16 = s_in
    nq = seq // BQ
    nk = seq // BK
    n = batch * seq
    nb, n_pages = page_table.shape
    assert n_pages % DEC_PAGES == 0 and (nb * (n_pages // DEC_PAGES)) % batch == 0
    gpb = nb * (n_pages // DEC_PAGES) // batch
    blocks = np.cumsum(np.arange(1, nq + 1))
    cum = np.round(gpb * blocks / blocks[-1]).astype(np.int64)
    cnt = np.diff(np.concatenate([[0], cum]))
    sched = jnp.asarray(np.concatenate([cum - cnt, cnt]), dtype=jnp.int32)
    pt = page_table.reshape(-1).astype(jnp.int32)

    qrow = lambda b, i, pt, sch: (b * nq + i, 0)

    def const(shape):
        nd = len(shape)
        return pl.BlockSpec(shape, lambda b, i, pt, sch: (0,) * nd, pipeline_mode=pl.Buffered(1))

    in_specs = [
        pl.BlockSpec((BQ, MLA_HEADS * QCAT), qrow),
        pl.BlockSpec((BQ, 512), qrow),
        pl.BlockSpec((seq, QCAT), lambda b, i, pt, sch: (b, 0), pipeline_mode=pl.Buffered(1)),
        pl.BlockSpec((nk, VT_ROWS, BK), lambda b, i, pt, sch: (b, 0, 0), pipeline_mode=pl.Buffered(1)),
        pl.BlockSpec((seq, 512), lambda b, i, pt, sch: (b, 0), pipeline_mode=pl.Buffered(1)),
        pl.BlockSpec((nk, DIFF_HEADS, VT_ROWS, BK), lambda b, i, pt, sch: (b, 0, 0, 0),
                     pipeline_mode=pl.Buffered(1)),
        const(bias.shape),
        const((1, DIFF_QK)), const((1, DIFF_QK)), const((1, DIFF_QK)), const((1, DIFF_QK)),
        const((DIFF_V, 1)), const((1, DIFF_V)),
        const((nb * MLA_HEADS, QCAT)), const((nb, 512)), const((nb, QCAT)), const((nb, 512)), const((nb, 512)),
        const(bias_dec.shape),
        const(bias_self.shape),
        pl.BlockSpec(memory_space=pl.ANY), pl.BlockSpec(memory_space=pl.ANY),
        pl.BlockSpec(memory_space=pl.ANY), pl.BlockSpec(memory_space=pl.ANY),
    ]
    out_specs = (pl.BlockSpec((BQ, MLA_HEADS * LANES), qrow), pl.BlockSpec((BQ, DIFF_HEADS * DIFF_V), qrow),
                 const((nb * MLA_HEADS, LANES)), const((nb * DIFF_HEADS, DIFF_V)))
    out_shape = (jax.ShapeDtypeStruct((n, MLA_HEADS * LANES), BF16),
                 jax.ShapeDtypeStruct((n, DIFF_HEADS * DIFF_V), BF16),
                 jax.ShapeDtypeStruct((nb * MLA_HEADS, LANES), F32),
                 jax.ShapeDtypeStruct((nb * DIFF_HEADS, DIFF_V), F32))
    nrow = 2 * DIFF_HEADS
    scratch = [
        pltpu.VMEM((NQ_MLA, QCAT), BF16),
        pltpu.VMEM((DIFF_HEADS, NQ_DIFF, LANES), BF16),
        pltpu.VMEM((1, NQ_ALL), F32), pltpu.VMEM((VT_ROWS, NQ_ALL), F32),
        pltpu.VMEM((N_SLOTS, GROUP_KEYS, MLA_KVR), F32),
        pltpu.VMEM((N_SLOTS, MLA_ROPE, GROUP_KEYS), F32),
        pltpu.VMEM((N_SLOTS, 512, GROUP_KEYS), F32),
        pltpu.VMEM((N_SLOTS, GROUP_KEYS * DIFF_HEADS, DIFF_V), F32),
        pltpu.SemaphoreType.DMA((N_SLOTS, 4)),
        pltpu.VMEM((GROUP_KEYS, MLA_KVR), BF16), pltpu.VMEM((MLA_ROPE, GROUP_KEYS), BF16),
        pltpu.VMEM((512, GROUP_KEYS), BF16), pltpu.VMEM((GROUP_KEYS, DIFF_HEADS * DIFF_V), BF16),
        pltpu.VMEM((MLA_HEADS, 1), F32), pltpu.VMEM((MLA_HEADS, 1), F32), pltpu.VMEM((MLA_HEADS, MLA_KVR), F32),
        pltpu.VMEM((nrow, 1), F32), pltpu.VMEM((nrow, 1), F32), pltpu.VMEM((nrow, 512), F32),
    ]
    grid_spec = pltpu.PrefetchScalarGridSpec(
        num_scalar_prefetch=2, grid=(batch, nq), in_specs=in_specs, out_specs=out_specs, scratch_shapes=scratch)
    olat, odiff, s_olat, s_odiff = pl.pallas_call(
        functools.partial(_fused_kernel, n_pages=n_pages, groups_per_batch=gpb),
        grid_spec=grid_spec,
        out_shape=out_shape,
        compiler_params=_cparams(("arbitrary", "arbitrary")),
        name="attn_fused",
    )(pt, sched, qcat, dq, kvcat, ckvt, dk16, dvt, bias, *lam_vecs, subg.reshape(-1, 1), subg.reshape(1, -1),
      s_qcat.reshape(nb * MLA_HEADS, QCAT).astype(F32), s_dq.astype(F32), s_kvcat.astype(F32),
      s_dk16.astype(F32), s_dv16.astype(F32), bias_dec, bias_self, *caches)
    return (olat, odiff, s_olat.reshape(nb, MLA_HEADS * LANES).astype(BF16),
            s_odiff.reshape(nb, DIFF_HEADS * DIFF_V).astype(BF16))


def _post_kernel(x_ref, olat_ref, odiff_ref, ga_ref, gb_ref, wf2_ref, wbd_ref, wo_ref,
                 gffn_ref, wg_ref, wu_ref, wd_ref, gfin_ref, y_ref):
    bm = jnp.dot(olat_ref[...], wf2_ref[...], preferred_element_type=F32)
    bd = jnp.dot(odiff_ref[...], wbd_ref[...], preferred_element_type=F32)
    merged = (ga_ref[...] * bm + gb_ref[...] * bd).astype(BF16)
    x = x_ref[...] + jnp.dot(merged, wo_ref[...], preferred_element_type=F32)
    h2 = _rms(x, gffn_ref[...], NORM_EPS).astype(BF16)
    gate = jnp.dot(h2, wg_ref[...], preferred_element_type=F32)
    up = jnp.dot(h2, wu_ref[...], preferred_element_type=F32)
    act = (gate * jax.nn.sigmoid(gate) * up).astype(BF16)
    x = x + jnp.dot(act, wd_ref[...], preferred_element_type=F32)
    y_ref[...] = _rms(x, gfin_ref[...], NORM_EPS)


def _post(x, olat, odiff, ga, gb, weights, tm):
    n = x.shape[0]
    row = lambda i: (i, 0)
    wf2, wbd, wo, gffn, wg, wu, wd, gfin = weights
    single = pl.Buffered(1)

    def wspec(shape):
        return pl.BlockSpec(shape, lambda i: (0, 0), pipeline_mode=single)

    in_specs = [
        pl.BlockSpec((tm, D_MODEL), row),
        pl.BlockSpec((tm, MLA_HEADS * LANES), row),
        pl.BlockSpec((tm, 512), row),
        pl.BlockSpec((tm, D_MODEL), row),
        pl.BlockSpec((tm, D_MODEL), row),
        wspec(wf2.shape), wspec(wbd.shape), wspec(wo.shape), wspec(gffn.shape),
        wspec(wg.shape), wspec(wu.shape), wspec(wd.shape), wspec(gfin.shape),
    ]
    return pl.pallas_call(
        _post_kernel,
        grid=(n // tm,),
        in_specs=in_specs,
        out_specs=pl.BlockSpec((tm, D_MODEL), row),
        out_shape=jax.ShapeDtypeStruct((n, D_MODEL), F32),
        compiler_params=_cparams(("parallel",)),
        name="merge_ffn",
    )(x, olat, odiff, ga, gb, wf2, wbd, wo, gffn, wg, wu, wd, gfin)


def kernel(x_prompt, x_sample, cache_mla_ckv, cache_mla_kpe, cache_diff_k, cache_diff_v, page_table, rel_bias_table, norm_attn_g, w_in, mla_q_norm_g, mla_w_uq, mla_kv_norm_g, mla_w_uk, mla_w_uv, diff_lambda_q1, diff_lambda_k1, diff_lambda_q2, diff_lambda_k2, diff_subln_g, w_branch_mla, w_branch_diff, w_o, norm_ffn_g, w_ffn_gate, w_ffn_up, w_ffn_down, norm_final_g):
    batch, seq, _ = x_prompt.shape
    nb, dec_seq, _ = x_sample.shape
    assert dec_seq == 1 and w_in.shape[0] == 1
    n_pages = page_table.shape[1]
    past_len = n_pages * PAGE
    assert REL_MAX_DIST <= BQ + 1 and seq % BQ == 0
    n_pool = cache_mla_ckv.shape[1]

    w = w_in[0]
    sw = np.concatenate([np.arange(16, 32), np.arange(0, 16)])
    z96 = jnp.zeros((D_MODEL, LANES - MLA_ROPE), F32)
    w_kpe = w[:, 512:544]
    w_in_r = jnp.concatenate(
        [w[:, 0:384], w[:, 384:512], w_kpe, z96, w_kpe[:, sw], z96,
         w[:, 544:1056] * DIFF_SCALE, w[:, 1056:1568], w[:, 1568:2080], w[:, 2080:3104], w[:, 3104:4128]],
        axis=1).astype(BF16)
    w_fold, w_f2 = _fold_weights(mla_w_uq[0], mla_w_uk[0], mla_w_uv[0], w_branch_mla[0])
    uq = mla_w_uq[0].reshape(MLA_QR, MLA_HEADS, MLA_NOPE + MLA_ROPE)
    w_pe = uq[:, :, MLA_NOPE:]
    w_q2 = jnp.concatenate([w_fold, w_pe.reshape(MLA_QR, -1), w_pe[:, :, sw].reshape(MLA_QR, -1)],
                           axis=1).astype(BF16)
    proj_w = (norm_attn_g[0].reshape(1, -1), w_in_r, mla_q_norm_g[0].reshape(1, -1), w_q2,
              mla_kv_norm_g[0].reshape(1, -1))
    post_w = (w_f2.astype(BF16), w_branch_diff[0].astype(BF16), w_o[0].astype(BF16),
              norm_ffn_g[0].reshape(1, -1), w_ffn_gate[0].astype(BF16), w_ffn_up[0].astype(BF16),
              w_ffn_down[0].astype(BF16), norm_final_g.reshape(1, -1))
    lam_vecs = tuple(v[0].reshape(1, -1) for v in (diff_lambda_q1, diff_lambda_k1, diff_lambda_q2, diff_lambda_k2))
    subg = diff_subln_g[0]

    tab_p = _rope_tables(jnp.arange(seq, dtype=jnp.int32))
    tab_s = _rope_tables(jnp.full((nb,), past_len, dtype=jnp.int32))
    bias_p, bias_dec, bias_self = _bias_tables(rel_bias_table, past_len)

    xp = x_prompt.reshape(batch * seq, D_MODEL)
    (p_ckv, p_kpe, p_dk, p_dv, p_kvcat, p_ckvt, p_qcat, p_dq, p_dk16, p_dvt, p_ga, p_gb) = _proj(
        xp, tab_p, BK, proj_w)
    xs = x_sample.reshape(nb, D_MODEL)
    (s_ckv, s_kpe, s_dk, s_dv, s_kvcat, _, s_qcat, s_dq, s_dk16, s_dvt, s_ga, s_gb) = _proj(
        xs, tab_s, nb, proj_w)

    caches = (cache_mla_ckv.reshape(n_pool, PAGE, MLA_KVR),
              jnp.transpose(cache_mla_kpe, (0, 1, 3, 2)).reshape(n_pool, MLA_ROPE, PAGE),
              jnp.transpose(cache_diff_k, (0, 1, 3, 4, 5, 2)).reshape(n_pool, 512, PAGE),
              cache_diff_v.reshape(n_pool, PAGE * DIFF_HEADS, DIFF_V))
    p_olat, p_odiff, s_olat, s_odiff = _attn_fused(
        (p_qcat, p_dq, p_kvcat, p_ckvt, p_dk16, p_dvt),
        (s_qcat, s_dq, s_kvcat, s_dk16, s_dv.reshape(nb, DIFF_HEADS * DIFF_V).astype(BF16)),
        caches, page_table, bias_p, bias_dec, bias_self, lam_vecs, subg, batch, seq)

    y_prompt = _post(xp, p_olat, p_odiff, p_ga, p_gb, post_w, 256).reshape(batch, seq, D_MODEL)
    y_sample = _post(xs, s_olat, s_odiff, s_ga, s_gb, post_w, nb).reshape(nb, 1, D_MODEL)

    def kpe_out(t, b, s):
        return jnp.transpose(t, (0, 2, 1)).reshape(1, b, s, MLA_ROPE)

    def dk_out(t, b, s):
        t = t.reshape(b, DIFF_HEADS, 2, DIFF_QK, s)
        return jnp.transpose(t, (0, 4, 1, 2, 3)).reshape(1, b, s, DIFF_HEADS, 2, DIFF_QK)

    return (y_prompt, y_sample,
            p_ckv.reshape(1, batch, seq, MLA_KVR), kpe_out(p_kpe, batch, seq),
            dk_out(p_dk, batch, seq), p_dv.reshape(1, batch, seq, DIFF_HEADS, DIFF_V),
            s_ckv.reshape(1, nb, 1, MLA_KVR), kpe_out(s_kpe, 1, nb).reshape(1, nb, 1, MLA_ROPE),
            dk_out(s_dk, 1, nb).reshape(1, nb, 1, DIFF_HEADS, 2, DIFF_QK),
            s_dv.reshape(1, nb, 1, DIFF_HEADS, DIFF_V))
```

```python
import functools
import math

import numpy as np
import jax
import jax.numpy as jnp
from jax import lax
from jax.experimental import pallas as pl
from jax.experimental.pallas import tpu as pltpu

F32 = jnp.float32
BF16 = jnp.bfloat16

D_MODEL = 1024
PAGE = 128
MLA_HEADS = 8
MLA_NOPE = 64
MLA_ROPE = 32
MLA_V = 64
MLA_QR = 384
MLA_KVR = 128
MLA_SCALE = (MLA_NOPE + MLA_ROPE) ** -0.5
ROPE_THETA = 10000.0
DIFF_HEADS = 4
DIFF_QK = 64
DIFF_V = 128
DIFF_SCALE = DIFF_QK ** -0.5
SUBLN_EPS = 1e-5
REL_BUCKETS = 32
REL_MAX_DIST = 128
FFN_DIM = 2816
NORM_EPS = 1e-6
LAM_INIT = 0.8 - 0.6 * math.exp(-0.3 * 0)

LANES = 128
QCAT = 2 * LANES
NEG = -1e30
LOG2E = math.log2(math.e)
VMEM_LIMIT = 56 * 1024 * 1024

C_QLAT = 0
C_KV = 384
C_KPE = 512
C_KPESW = 640
C_DQ = 768
C_DK = 1280
C_DV = 1792
C_GA = 2304
C_GB = 3328
C_END = 4352

BQ = 256
BK = 256
DEC_PAGES = 16
VT_ROWS = MLA_KVR + 16


def _cparams(sem):
    return pltpu.CompilerParams(dimension_semantics=sem, vmem_limit_bytes=VMEM_LIMIT)


def _const_spec(shape):
    nd = len(shape)
    return pl.BlockSpec(shape, lambda *_: (0,) * nd, pipeline_mode=pl.Buffered(1))


def _rope_tables_kernel(pos_ref, inv_ref, cq_ref, sq_ref, ck_ref, sk_ref):
    ang = pos_ref[...] * inv_ref[...]
    c = jnp.cos(ang)
    s = jnp.sin(ang)
    lane = lax.broadcasted_iota(jnp.int32, ang.shape, 1)
    first_half = (lane & (MLA_ROPE - 1)) < (MLA_ROPE // 2)
    s_signed = jnp.where(first_half, -s, s)
    cq_ref[...] = c
    sq_ref[...] = s_signed
    kmask = lax.broadcasted_iota(jnp.int32, ck_ref.shape, 1) < MLA_ROPE
    ck_ref[...] = jnp.where(kmask, c[:, :LANES], 0.0)
    sk_ref[...] = jnp.where(kmask, s_signed[:, :LANES], 0.0)


def _rope_tables(pos):
    n = pos.shape[0]
    half = MLA_ROPE // 2
    inv = (ROPE_THETA ** (-jnp.arange(half, dtype=F32) / half))
    inv = jnp.tile(inv, 2 * MLA_HEADS).reshape(1, 2 * LANES)
    return pl.pallas_call(
        _rope_tables_kernel,
        out_shape=(jax.ShapeDtypeStruct((n, 2 * LANES), F32), jax.ShapeDtypeStruct((n, 2 * LANES), F32),
                   jax.ShapeDtypeStruct((n, LANES), F32), jax.ShapeDtypeStruct((n, LANES), F32)),
        name="rope_tables",
    )(pos.astype(F32).reshape(n, 1), inv)


def _bucket(n):
    max_exact = REL_BUCKETS // 2
    nf = jnp.maximum(n, 1).astype(F32)
    large = max_exact + (jnp.log(nf / max_exact) / math.log(REL_MAX_DIST / max_exact)
                         * (REL_BUCKETS - max_exact)).astype(jnp.int32)
    large = jnp.minimum(large, REL_BUCKETS - 1)
    return jnp.where(n < max_exact, n, large)


def _bias_kernel(tab_ref, bp_ref, bd_ref, bs_ref, *, past_len):
    kidx = lax.broadcasted_iota(jnp.int32, (BK, BQ), 0)
    qidx = lax.broadcasted_iota(jnp.int32, (BK, BQ), 1)
    for d in range(3):
        rel = d * BQ + qidx - kidx
        b = _bucket(jnp.maximum(rel, 0))
        for h in range(DIFF_HEADS):
            val = jnp.zeros((BK, BQ), F32)
            for t in range(REL_BUCKETS):
                val = jnp.where(b == t, tab_ref[t, h] * LOG2E, val)
            if d == 0:
                val = jnp.where(rel >= 0, val, NEG)
            bp_ref[d * DIFF_HEADS + h] = val
    bp_ref[3 * DIFF_HEADS] = jnp.where(qidx - kidx >= 0, 0.0, NEG)
    gk = DEC_PAGES * PAGE
    row = lax.broadcasted_iota(jnp.int32, (2 * DIFF_HEADS, gk), 0)
    for g in range(past_len // gk):
        kpos = g * gk + lax.broadcasted_iota(jnp.int32, (2 * DIFF_HEADS, gk), 1)
        b = _bucket(jnp.maximum(past_len - kpos, 0))
        val = jnp.zeros((2 * DIFF_HEADS, gk), F32)
        for h in range(DIFF_HEADS):
            for t in range(REL_BUCKETS):
                val = jnp.where((b == t) & ((row >> 1) == h), tab_ref[t, h] * LOG2E, val)
        bd_ref[g] = val
    rows = lax.broadcasted_iota(jnp.int32, (2 * DIFF_HEADS, LANES), 0)
    b0 = _bucket(jnp.zeros((2 * DIFF_HEADS, LANES), jnp.int32))
    vs = jnp.zeros((2 * DIFF_HEADS, LANES), F32)
    for h in range(DIFF_HEADS):
        for t in range(REL_BUCKETS):
            vs = jnp.where((b0 == t) & ((rows >> 1) == h), tab_ref[t, h] * LOG2E, vs)
    bs_ref[...] = vs


def _bias_tables(table, past_len):
    return pl.pallas_call(
        functools.partial(_bias_kernel, past_len=past_len),
        in_specs=[pl.BlockSpec(memory_space=pltpu.SMEM)],
        out_shape=(jax.ShapeDtypeStruct((3 * DIFF_HEADS + 1, BK, BQ), F32),
                   jax.ShapeDtypeStruct((past_len // (DEC_PAGES * PAGE), 2 * DIFF_HEADS, DEC_PAGES * PAGE), F32),
                   jax.ShapeDtypeStruct((2 * DIFF_HEADS, LANES), F32)),
        compiler_params=pltpu.CompilerParams(vmem_limit_bytes=VMEM_LIMIT),
        name="bias_tables",
    )(table.astype(F32))


def _fold_kernel(nope_ref, uk_ref, uv_ref, bm_ref, wfold_ref, wf2_ref):
    for h in range(MLA_HEADS):
        wfold_ref[:, h * LANES:(h + 1) * LANES] = lax.dot_general(
            nope_ref[h], uk_ref[h], (((1,), (1,)), ((), ())),
            preferred_element_type=F32, precision=lax.Precision.HIGHEST)
        wf2_ref[h * LANES:(h + 1) * LANES, :] = jnp.dot(
            uv_ref[h], bm_ref[h], preferred_element_type=F32, precision=lax.Precision.HIGHEST)


def _fold_weights(w_uq, w_uk, w_uv, w_bm):
    uq = w_uq.reshape(MLA_QR, MLA_HEADS, MLA_NOPE + MLA_ROPE)
    nope = jnp.transpose(uq[:, :, :MLA_NOPE], (1, 0, 2))
    uk = jnp.transpose(w_uk, (1, 0, 2))
    uv = jnp.transpose(w_uv, (1, 0, 2))
    bm = w_bm.reshape(MLA_HEADS, MLA_V, D_MODEL)
    return pl.pallas_call(
        _fold_kernel,
        out_shape=(jax.ShapeDtypeStruct((MLA_QR, MLA_HEADS * LANES), F32),
                   jax.ShapeDtypeStruct((MLA_HEADS * LANES, D_MODEL), F32)),
        compiler_params=pltpu.CompilerParams(vmem_limit_bytes=VMEM_LIMIT),
        name="fold_weights",
    )(nope, uk, uv, bm)


def _rms(x, g, eps):
    return x * lax.rsqrt(jnp.mean(x * x, axis=-1, keepdims=True) + eps) * g


def _proj_kernel(x_ref, g_ref, win_ref, gq_ref, wq2_ref, gkv_ref, cq_ref, sq_ref, ck_ref, sk_ref,
                 ckv_ref, kpet_ref, dkt_ref, dv_ref, kvcat_ref, ckvt_ref, qcat_ref, dq_ref, dk16_ref, dvt_ref,
                 ga_ref, gb_ref):
    tm = x_ref.shape[0]
    h = _rms(x_ref[...], g_ref[...], NORM_EPS).astype(BF16)

    def seg(lo, hi):
        return jnp.dot(h, win_ref[:, lo:hi], preferred_element_type=F32)

    z0 = seg(C_QLAT, C_KPE)
    qn = _rms(z0[:, :C_KV], gq_ref[...], NORM_EPS).astype(BF16)
    zq = jnp.dot(qn, wq2_ref[...], preferred_element_type=F32)
    nq = MLA_HEADS * LANES
    qscale = MLA_SCALE * LOG2E
    qpe = (zq[:, nq:nq + 2 * LANES] * cq_ref[...] + zq[:, nq + 2 * LANES:] * sq_ref[...]) * qscale
    lane = lax.broadcasted_iota(jnp.int32, (tm, LANES), 1)
    per_block = LANES // MLA_ROPE
    for hd in range(MLA_HEADS):
        blk = qpe[:, (hd // per_block) * LANES:(hd // per_block + 1) * LANES]
        sh = (hd % per_block) * MLA_ROPE
        if sh:
            blk = pltpu.roll(blk, LANES - sh, axis=1)
        blk = jnp.where(lane < MLA_ROPE, blk, 0.0)
        qa16 = (zq[:, hd * LANES:(hd + 1) * LANES] * qscale).astype(BF16)
        qp16 = blk.astype(BF16)
        for c in range(qcat_ref.shape[0]):
            rows = slice(hd * (tm // qcat_ref.shape[0]), (hd + 1) * (tm // qcat_ref.shape[0]))
            tok = slice(c * (tm // qcat_ref.shape[0]), (c + 1) * (tm // qcat_ref.shape[0]))
            qcat_ref[c, rows, 0:LANES] = qa16[tok]
            qcat_ref[c, rows, LANES:QCAT] = qp16[tok]

    ckv = _rms(z0[:, C_KV:C_KPE], gkv_ref[...], NORM_EPS)
    zk = seg(C_KPE, C_DQ)
    kpe = zk[:, :LANES] * ck_ref[...] + zk[:, LANES:] * sk_ref[...]
    ckv_ref[...] = ckv
    kpet_ref[0] = kpe.T[:MLA_ROPE]
    kvcat_ref[:, :LANES] = ckv.astype(BF16)
    kvcat_ref[:, LANES:] = kpe.astype(BF16)
    nkb = ckvt_ref.shape[0]
    kb = tm // nkb
    ones_rows = jnp.ones((VT_ROWS - MLA_KVR, kb), BF16)
    for c in range(nkb):
        ckvt_ref[c, 0:MLA_KVR, :] = ckv[c * kb:(c + 1) * kb, :].T.astype(BF16)
        ckvt_ref[c, MLA_KVR:VT_ROWS, :] = ones_rows

    dq_ref[...] = (seg(C_DQ, C_DK) * LOG2E).astype(BF16)
    dk = seg(C_DK, C_DV)
    dkt_ref[0] = dk.T
    dk16_ref[...] = dk.astype(BF16)
    dv = seg(C_DV, C_GA)
    for hd in range(DIFF_HEADS):
        dv_ref[pl.ds(hd, tm, stride=DIFF_HEADS), :] = dv[:, hd * DIFF_V:(hd + 1) * DIFF_V]
    for c in range(nkb):
        for hd in range(DIFF_HEADS):
            dvt_ref[c, hd, 0:DIFF_V, :] = dv[c * kb:(c + 1) * kb, hd * DIFF_V:(hd + 1) * DIFF_V].T.astype(BF16)
            dvt_ref[c, hd, DIFF_V:VT_ROWS, :] = ones_rows

    ga_ref[...] = jax.nn.sigmoid(seg(C_GA, C_GB))
    gb_ref[...] = jax.nn.sigmoid(seg(C_GB, C_END))


def _proj(x, tables, tm, weights, kb=None):
    g_attn, w_in_r, g_q, w_q2, g_kv = weights
    cq, sq, ck, sk = tables
    n = x.shape[0]
    nb = n // tm
    kb = tm if kb is None else kb
    nkb = tm // kb
    seq = cq.shape[0]
    tb = seq // tm
    batch = n // seq
    tcol = lambda i: (i // tb, 0, i % tb)
    row = lambda i: (i, 0)
    trow = lambda i: (i % tb, 0)
    in_specs = [
        pl.BlockSpec((tm, D_MODEL), row),
        _const_spec((1, D_MODEL)),
        _const_spec((D_MODEL, C_END)),
        _const_spec((1, MLA_QR)),
        _const_spec((MLA_QR, w_q2.shape[1])),
        _const_spec((1, MLA_KVR)),
        pl.BlockSpec((tm, 2 * LANES), trow), pl.BlockSpec((tm, 2 * LANES), trow),
        pl.BlockSpec((tm, LANES), trow), pl.BlockSpec((tm, LANES), trow),
    ]
    out_shape = (
        jax.ShapeDtypeStruct((n, MLA_KVR), F32),
        jax.ShapeDtypeStruct((batch, MLA_ROPE, seq), F32),
        jax.ShapeDtypeStruct((batch, 2 * DIFF_HEADS * DIFF_QK, seq), F32),
        jax.ShapeDtypeStruct((n * DIFF_HEADS, DIFF_V), F32),
        jax.ShapeDtypeStruct((n, QCAT), BF16),
        jax.ShapeDtypeStruct((nb * nkb, VT_ROWS, kb), BF16),
        jax.ShapeDtypeStruct((nb * nkb, MLA_HEADS * kb, QCAT), BF16),
        jax.ShapeDtypeStruct((n, 2 * DIFF_HEADS * DIFF_QK), BF16),
        jax.ShapeDtypeStruct((n, 2 * DIFF_HEADS * DIFF_QK), BF16),
        jax.ShapeDtypeStruct((nb * nkb, DIFF_HEADS, VT_ROWS, kb), BF16),
        jax.ShapeDtypeStruct((n, D_MODEL), F32),
        jax.ShapeDtypeStruct((n, D_MODEL), F32),
    )
    out_specs = (
        pl.BlockSpec((tm, MLA_KVR), row),
        pl.BlockSpec((1, MLA_ROPE, tm), tcol),
        pl.BlockSpec((1, 512, tm), tcol),
        pl.BlockSpec((tm * DIFF_HEADS, DIFF_V), row),
        pl.BlockSpec((tm, QCAT), row),
        pl.BlockSpec((nkb, VT_ROWS, kb), lambda i: (i, 0, 0)),
        pl.BlockSpec((nkb, MLA_HEADS * kb, QCAT), lambda i: (i, 0, 0)),
        pl.BlockSpec((tm, 512), row),
        pl.BlockSpec((tm, 512), row),
        pl.BlockSpec((nkb, DIFF_HEADS, VT_ROWS, kb), lambda i: (i, 0, 0, 0)),
        pl.BlockSpec((tm, D_MODEL), row),
        pl.BlockSpec((tm, D_MODEL), row),
    )
    return pl.pallas_call(
        _proj_kernel,
        grid=(nb,),
        in_specs=in_specs,
        out_specs=out_specs,
        out_shape=out_shape,
        compiler_params=_cparams(("parallel",)),
        name="proj_in",
    )(x, g_attn, w_in_r, g_q, w_q2, g_kv, cq, sq, ck, sk)


def _lambda(lq1_ref, lk1_ref, lq2_ref, lk2_ref):
    a = jnp.sum(lq1_ref[...] * lk1_ref[...], axis=-1, keepdims=True)
    b = jnp.sum(lq2_ref[...] * lk2_ref[...], axis=-1, keepdims=True)
    return jnp.exp(a) - jnp.exp(b) + LAM_INIT


NQ_MLA = MLA_HEADS * BQ
NQ_DIFF = 2 * BQ
NQ_ALL = NQ_MLA + DIFF_HEADS * NQ_DIFF
GROUP_KEYS = DEC_PAGES * PAGE
N_SLOTS = 2


def _fused_kernel(pt_ref, sch_ref,
                  qcat_ref, dq_ref, kv_ref, ckvt_ref, dk_ref, dvt_ref, bias_ref,
                  lq1_ref, lk1_ref, lq2_ref, lk2_ref, subg_ref, subgr_ref,
                  sq_ref, sdq_ref, skv_ref, sdk_ref, sdv_ref, bd_ref, bs_ref,
                  cckv_hbm, ckpe_hbm, cdk_hbm, cdv_hbm,
                  olat_ref, odiff_ref, solat_ref, sodiff_ref,
                  qd_scr, m_scr, acc_scr,
                  ckv_buf, kpe_buf, dkt_buf, dv_buf, sem, ckv16_scr, kpe16_scr, dkt16_scr, v16_scr,
                  smm_scr, slm_scr, sam_scr, smd_scr, sld_scr, sad_scr,
                  *, n_pages, groups_per_batch):
    b = pl.program_id(0)
    i = pl.program_id(1)
    nq = pl.num_programs(1)
    step = b * nq + i
    n_steps = pl.num_programs(0) * nq
    groups_per_row = n_pages // DEC_PAGES
    n_total = pl.num_programs(0) * groups_per_batch
    g_base = b * groups_per_batch + sch_ref[i]
    n_dec = sch_ref[nq + i]
    nrow = 2 * DIFF_HEADS

    def group_copies(row, grp, slot):
        cps = []
        for k in range(DEC_PAGES):
            pg = pt_ref[row * n_pages + grp * DEC_PAGES + k]
            keys = pl.ds(k * PAGE, PAGE)
            cps.append(pltpu.make_async_copy(cckv_hbm.at[pg], ckv_buf.at[slot, keys, :], sem.at[slot, 0]))
            cps.append(pltpu.make_async_copy(ckpe_hbm.at[pg], kpe_buf.at[slot, :, keys], sem.at[slot, 1]))
            cps.append(pltpu.make_async_copy(cdk_hbm.at[pg], dkt_buf.at[slot, :, keys], sem.at[slot, 2]))
            cps.append(pltpu.make_async_copy(cdv_hbm.at[pg],
                                             dv_buf.at[slot, pl.ds(k * PAGE * DIFF_HEADS, PAGE * DIFF_HEADS), :],
                                             sem.at[slot, 3]))
        return cps

    def start_group(row, grp, slot):
        for c in group_copies(row, grp, slot):
            c.start()

    def wait_group(row, grp, slot):
        for c in group_copies(row, grp, slot):
            c.wait()

    @pl.when(step == 0)
    def _():
        start_group(0, 0, 0)
        start_group(1 // groups_per_row, 1 % groups_per_row, 1)

    def decode_group(g):
        gl = g_base + g
        row = gl // groups_per_row
        grp = gl % groups_per_row
        slot = gl % N_SLOTS
        wait_group(row, grp, slot)

        ckv16_scr[...] = ckv_buf[slot].astype(BF16)
        kpe16_scr[...] = kpe_buf[slot].astype(BF16)
        dkt16_scr[...] = dkt_buf[slot].astype(BF16)
        for hd in range(DIFF_HEADS):
            v16_scr[:, hd * DIFF_V:(hd + 1) * DIFF_V] = dv_buf[
                slot, pl.ds(hd, GROUP_KEYS, stride=DIFF_HEADS), :].astype(BF16)
        gn = jnp.minimum(gl + N_SLOTS, n_total - 1)
        start_group(gn // groups_per_row, gn % groups_per_row, slot)

        q = sq_ref[pl.ds(row * MLA_HEADS, MLA_HEADS), :].astype(BF16)
        qa = q[:, :MLA_KVR]
        qp = q[:, MLA_KVR:MLA_KVR + MLA_ROPE]
        rowi = lax.broadcasted_iota(jnp.int32, (nrow, 512), 0)
        lanei = lax.broadcasted_iota(jnp.int32, (nrow, 512), 1)
        dq_b = jnp.broadcast_to(sdq_ref[pl.ds(row, 1), :], (nrow, 512))
        qbd = jnp.where((lanei >> 6) == rowi, dq_b, 0.0).astype(BF16)

        first = grp == 0
        kvs = skv_ref[pl.ds(row, 1), :]
        smm_scr[...] = jnp.where(first, jnp.sum(q.astype(F32) * kvs, axis=1, keepdims=True), smm_scr[...])
        slm_scr[...] = jnp.where(first, 1.0, slm_scr[...])
        sam_scr[...] = jnp.where(first, jnp.broadcast_to(kvs[:, :MLA_KVR], sam_scr.shape), sam_scr[...])
        s0 = jnp.sum(qbd.astype(F32) * sdk_ref[pl.ds(row, 1), :], axis=1, keepdims=True)
        smd_scr[...] = jnp.where(first, s0 + bs_ref[:, 0:1], smd_scr[...])
        sld_scr[...] = jnp.where(first, 1.0, sld_scr[...])
        sad_scr[...] = jnp.where(first, jnp.broadcast_to(sdv_ref[pl.ds(row, 1), :], sad_scr.shape), sad_scr[...])

        tdims = (((1,), (1,)), ((), ()))
        ckv16 = ckv16_scr[...]
        sm = (lax.dot_general(qa, ckv16, tdims, preferred_element_type=F32)
              + jnp.dot(qp, kpe16_scr[...], preferred_element_type=F32))
        sd = jnp.dot(qbd, dkt16_scr[...], preferred_element_type=F32) + bd_ref[grp]

        def merge(s, m_scr, l_scr):
            m_old = m_scr[...]
            m_new = jnp.maximum(m_old, jnp.max(s, axis=1, keepdims=True))
            a = jnp.exp2(m_old - m_new)
            p = jnp.exp2(s - m_new)
            l_scr[...] = l_scr[...] * a + jnp.sum(p, axis=1, keepdims=True)
            m_scr[...] = m_new
            return a, p.astype(BF16)

        a, p = merge(sm, smm_scr, slm_scr)
        sam_scr[...] = sam_scr[...] * a + jnp.dot(p, ckv16, preferred_element_type=F32)
        a, p = merge(sd, smd_scr, sld_scr)
        sad_scr[...] = sad_scr[...] * a + jnp.dot(p, v16_scr[...], preferred_element_type=F32)

        solat_ref[pl.ds(row * MLA_HEADS, MLA_HEADS), :] = sam_scr[...] / slm_scr[...]
        lam = _lambda(lq1_ref, lk1_ref, lq2_ref, lk2_ref)
        o = sad_scr[...] / sld_scr[...]
        for hd in range(DIFF_HEADS):
            o1 = o[2 * hd:2 * hd + 1, hd * DIFF_V:(hd + 1) * DIFF_V]
            o2 = o[2 * hd + 1:2 * hd + 2, hd * DIFF_V:(hd + 1) * DIFF_V]
            y = _rms(o1 - lam * o2, subgr_ref[...], SUBLN_EPS) * (1.0 - LAM_INIT)
            sodiff_ref[pl.ds(row * DIFF_HEADS + hd, 1), :] = y

    lane = lax.broadcasted_iota(jnp.int32, (BQ, LANES), 1)
    for hd in range(DIFF_HEADS):
        dqh = dq_ref[:, hd * LANES:(hd + 1) * LANES]
        qd_scr[hd, 0:BQ, :] = jnp.where(lane < DIFF_QK, dqh, jnp.zeros_like(dqh))
        qd_scr[hd, BQ:2 * BQ, :] = jnp.where(lane >= DIFF_QK, dqh, jnp.zeros_like(dqh))

    m_scr[...] = jnp.full(m_scr.shape, NEG, F32)
    acc_scr[...] = jnp.zeros(acc_scr.shape, F32)
    tdims = (((1,), (1,)), ((), ()))

    def block(j, diag):
        koff = pl.multiple_of(j * BK, BK)
        st = lax.dot_general(kv_ref[pl.ds(koff, BK), :], qcat_ref[0], tdims, preferred_element_type=F32)
        if diag:
            mask = bias_ref[3 * DIFF_HEADS]
            st = st + jnp.concatenate([mask] * MLA_HEADS, axis=1)
        d = 0 if diag else jnp.minimum(i - j, 2)
        parts = [st]
        for hd in range(DIFF_HEADS):
            kb = dk_ref[pl.ds(koff, BK), hd * LANES:(hd + 1) * LANES]
            b1 = bias_ref[d * DIFF_HEADS + hd]
            sh = lax.dot_general(kb, qd_scr[hd], tdims, preferred_element_type=F32)
            parts.append(sh + jnp.concatenate([b1, b1], axis=1))
        st = jnp.concatenate(parts, axis=1)
        m_old = m_scr[...]
        m_new = jnp.maximum(m_old, jnp.max(st, axis=0, keepdims=True))
        a = jnp.exp2(m_old - m_new)
        pt = jnp.exp2(st - m_new).astype(BF16)
        m_scr[...] = m_new
        cs = slice(0, NQ_MLA)
        acc_scr[:, cs] = acc_scr[:, cs] * a[:, cs] + jnp.dot(ckvt_ref[j], pt[:, cs], preferred_element_type=F32)
        for hd in range(DIFF_HEADS):
            cs = slice(NQ_MLA + hd * NQ_DIFF, NQ_MLA + (hd + 1) * NQ_DIFF)
            acc_scr[:, cs] = acc_scr[:, cs] * a[:, cs] + jnp.dot(dvt_ref[j, hd], pt[:, cs],
                                                               preferred_element_type=F32)

    def paired(j, carry):
        decode_group(j)
        block(j, False)
        return carry

    def alone(j, carry):
        block(j, False)
        return carry

    def tail(j, carry):
        decode_group(j)
        return carry

    n_paired = jnp.minimum(i, n_dec)
    lax.fori_loop(0, n_paired, paired, 0)
    lax.fori_loop(n_paired, i, alone, 0)

    @pl.when(i < n_dec)
    def _():
        decode_group(i)

    block(i, True)
    lax.fori_loop(i + 1, n_dec, tail, 0)

    @pl.when(step == n_steps - 1)
    def _():
        last = n_total - 1
        for s_ in range(N_SLOTS):
            wait_group(last // groups_per_row, last % groups_per_row, s_)

    acc = acc_scr[...]
    o = acc[:MLA_KVR] / acc[MLA_KVR:MLA_KVR + 1]
    for hd in range(MLA_HEADS):
        olat_ref[:, hd * LANES:(hd + 1) * LANES] = o[:, hd * BQ:(hd + 1) * BQ].T.astype(BF16)
    lam = _lambda(lq1_ref, lk1_ref, lq2_ref, lk2_ref)
    for hd in range(DIFF_HEADS):
        c0 = NQ_MLA + hd * NQ_DIFF
        od = o[:, c0:c0 + BQ] - lam * o[:, c0 + BQ:c0 + 2 * BQ]
        y = od * lax.rsqrt(jnp.mean(od * od, axis=0, keepdims=True) + SUBLN_EPS) * subg_ref[...]
        odiff_ref[:, hd * LANES:(hd + 1) * LANES] = (y * (1.0 - LAM_INIT)).T.astype(BF16)


def _attn_fused(p_in, s_in, caches, page_table, bias, bias_dec, bias_self, lam_vecs, subg, batch, seq):
    qcat, dq, kvcat, ckvt, dk16, dvt = p_in
    s_qcat, s_dq, s_kvcat, s_dk16, s_dv16 = s_in
    nq = seq // BQ
    nk = seq // BK
    n = batch * seq
    nb, n_pages = page_table.shape
    assert n_pages % DEC_PAGES == 0 and (nb * (n_pages // DEC_PAGES)) % batch == 0
    gpb = nb * (n_pages // DEC_PAGES) // batch
    blocks = np.cumsum(np.arange(1, nq + 1))
    cum = np.round(gpb * blocks / blocks[-1]).astype(np.int64)
    cnt = np.diff(np.concatenate([[0], cum]))
    sched = jnp.asarray(np.concatenate([cum - cnt, cnt]), dtype=jnp.int32)
    pt = page_table.reshape(-1).astype(jnp.int32)

    qrow = lambda b, i, pt, sch: (b * nq + i, 0)

    def const(shape):
        nd = len(shape)
        return pl.BlockSpec(shape, lambda b, i, pt, sch: (0,) * nd, pipeline_mode=pl.Buffered(1))

    in_specs = [
        pl.BlockSpec((1, NQ_MLA, QCAT), lambda b, i, pt, sch: (b * nq + i, 0, 0)),
        pl.BlockSpec((BQ, 512), qrow),
        pl.BlockSpec((seq, QCAT), lambda b, i, pt, sch: (b, 0), pipeline_mode=pl.Buffered(1)),
        pl.BlockSpec((nk, VT_ROWS, BK), lambda b, i, pt, sch: (b, 0, 0), pipeline_mode=pl.Buffered(1)),
        pl.BlockSpec((seq, 512), lambda b, i, pt, sch: (b, 0), pipeline_mode=pl.Buffered(1)),
        pl.BlockSpec((nk, DIFF_HEADS, VT_ROWS, BK), lambda b, i, pt, sch: (b, 0, 0, 0),
                     pipeline_mode=pl.Buffered(1)),
        const(bias.shape),
        const((1, DIFF_QK)), const((1, DIFF_QK)), const((1, DIFF_QK)), const((1, DIFF_QK)),
        const((DIFF_V, 1)), const((1, DIFF_V)),
        const((nb * MLA_HEADS, QCAT)), const((nb, 512)), const((nb, QCAT)), const((nb, 512)), const((nb, 512)),
        const(bias_dec.shape),
        const(bias_self.shape),
        pl.BlockSpec(memory_space=pl.ANY), pl.BlockSpec(memory_space=pl.ANY),
        pl.BlockSpec(memory_space=pl.ANY), pl.BlockSpec(memory_space=pl.ANY),
    ]
    out_specs = (pl.BlockSpec((BQ, MLA_HEADS * LANES), qrow), pl.BlockSpec((BQ, DIFF_HEADS * DIFF_V), qrow),
                 const((nb * MLA_HEADS, LANES)), const((nb * DIFF_HEADS, DIFF_V)))
    out_shape = (jax.ShapeDtypeStruct((n, MLA_HEADS * LANES), BF16),
                 jax.ShapeDtypeStruct((n, DIFF_HEADS * DIFF_V), BF16),
                 jax.ShapeDtypeStruct((nb * MLA_HEADS, LANES), F32),
                 jax.ShapeDtypeStruct((nb * DIFF_HEADS, DIFF_V), F32))
    nrow = 2 * DIFF_HEADS
    scratch = [
        pltpu.VMEM((DIFF_HEADS, NQ_DIFF, LANES), BF16),
        pltpu.VMEM((1, NQ_ALL), F32), pltpu.VMEM((VT_ROWS, NQ_ALL), F32),
        pltpu.VMEM((N_SLOTS, GROUP_KEYS, MLA_KVR), F32),
        pltpu.VMEM((N_SLOTS, MLA_ROPE, GROUP_KEYS), F32),
        pltpu.VMEM((N_SLOTS, 512, GROUP_KEYS), F32),
        pltpu.VMEM((N_SLOTS, GROUP_KEYS * DIFF_HEADS, DIFF_V), F32),
        pltpu.SemaphoreType.DMA((N_SLOTS, 4)),
        pltpu.VMEM((GROUP_KEYS, MLA_KVR), BF16), pltpu.VMEM((MLA_ROPE, GROUP_KEYS), BF16),
        pltpu.VMEM((512, GROUP_KEYS), BF16), pltpu.VMEM((GROUP_KEYS, DIFF_HEADS * DIFF_V), BF16),
        pltpu.VMEM((MLA_HEADS, 1), F32), pltpu.VMEM((MLA_HEADS, 1), F32), pltpu.VMEM((MLA_HEADS, MLA_KVR), F32),
        pltpu.VMEM((nrow, 1), F32), pltpu.VMEM((nrow, 1), F32), pltpu.VMEM((nrow, 512), F32),
    ]
    grid_spec = pltpu.PrefetchScalarGridSpec(
        num_scalar_prefetch=2, grid=(batch, nq), in_specs=in_specs, out_specs=out_specs, scratch_shapes=scratch)
    olat, odiff, s_olat, s_odiff = pl.pallas_call(
        functools.partial(_fused_kernel, n_pages=n_pages, groups_per_batch=gpb),
        grid_spec=grid_spec,
        out_shape=out_shape,
        compiler_params=_cparams(("arbitrary", "arbitrary")),
        name="attn_fused",
    )(pt, sched, qcat, dq, kvcat, ckvt, dk16, dvt, bias, *lam_vecs, subg.reshape(-1, 1), subg.reshape(1, -1),
      jnp.transpose(s_qcat.reshape(MLA_HEADS, nb, QCAT), (1, 0, 2)).reshape(nb * MLA_HEADS, QCAT).astype(F32), s_dq.astype(F32), s_kvcat.astype(F32),
      s_dk16.astype(F32), s_dv16.astype(F32), bias_dec, bias_self, *caches)
    return (olat, odiff, s_olat.reshape(nb, MLA_HEADS * LANES).astype(BF16),
            s_odiff.reshape(nb, DIFF_HEADS * DIFF_V).astype(BF16))


def _post_kernel(x_ref, olat_ref, odiff_ref, ga_ref, gb_ref, wf2_ref, wbd_ref, wo_ref,
                 gffn_ref, wg_ref, wu_ref, wd_ref, gfin_ref, y_ref):
    bm = jnp.dot(olat_ref[...], wf2_ref[...], preferred_element_type=F32)
    bd = jnp.dot(odiff_ref[...], wbd_ref[...], preferred_element_type=F32)
    merged = (ga_ref[...] * bm + gb_ref[...] * bd).astype(BF16)
    x = x_ref[...] + jnp.dot(merged, wo_ref[...], preferred_element_type=F32)
    h2 = _rms(x, gffn_ref[...], NORM_EPS).astype(BF16)
    gate = jnp.dot(h2, wg_ref[...], preferred_element_type=F32)
    up = jnp.dot(h2, wu_ref[...], preferred_element_type=F32)
    act = (gate * jax.nn.sigmoid(gate) * up).astype(BF16)
    x = x + jnp.dot(act, wd_ref[...], preferred_element_type=F32)
    y_ref[...] = _rms(x, gfin_ref[...], NORM_EPS)


def _post(x, olat, odiff, ga, gb, weights, tm):
    n = x.shape[0]
    row = lambda i: (i, 0)
    wf2, wbd, wo, gffn, wg, wu, wd, gfin = weights
    single = pl.Buffered(1)

    def wspec(shape):
        return pl.BlockSpec(shape, lambda i: (0, 0), pipeline_mode=single)

    in_specs = [
        pl.BlockSpec((tm, D_MODEL), row),
        pl.BlockSpec((tm, MLA_HEADS * LANES), row),
        pl.BlockSpec((tm, 512), row),
        pl.BlockSpec((tm, D_MODEL), row),
        pl.BlockSpec((tm, D_MODEL), row),
        wspec(wf2.shape), wspec(wbd.shape), wspec(wo.shape), wspec(gffn.shape),
        wspec(wg.shape), wspec(wu.shape), wspec(wd.shape), wspec(gfin.shape),
    ]
    return pl.pallas_call(
        _post_kernel,
        grid=(n // tm,),
        in_specs=in_specs,
        out_specs=pl.BlockSpec((tm, D_MODEL), row),
        out_shape=jax.ShapeDtypeStruct((n, D_MODEL), F32),
        compiler_params=_cparams(("parallel",)),
        name="merge_ffn",
    )(x, olat, odiff, ga, gb, wf2, wbd, wo, gffn, wg, wu, wd, gfin)


def kernel(x_prompt, x_sample, cache_mla_ckv, cache_mla_kpe, cache_diff_k, cache_diff_v, page_table, rel_bias_table, norm_attn_g, w_in, mla_q_norm_g, mla_w_uq, mla_kv_norm_g, mla_w_uk, mla_w_uv, diff_lambda_q1, diff_lambda_k1, diff_lambda_q2, diff_lambda_k2, diff_subln_g, w_branch_mla, w_branch_diff, w_o, norm_ffn_g, w_ffn_gate, w_ffn_up, w_ffn_down, norm_final_g):
    batch, seq, _ = x_prompt.shape
    nb, dec_seq, _ = x_sample.shape
    assert dec_seq == 1 and w_in.shape[0] == 1
    n_pages = page_table.shape[1]
    past_len = n_pages * PAGE
    assert REL_MAX_DIST <= BQ + 1 and seq % BQ == 0
    n_pool = cache_mla_ckv.shape[1]

    w = w_in[0]
    sw = np.concatenate([np.arange(16, 32), np.arange(0, 16)])
    z96 = jnp.zeros((D_MODEL, LANES - MLA_ROPE), F32)
    w_kpe = w[:, 512:544]
    w_in_r = jnp.concatenate(
        [w[:, 0:384], w[:, 384:512], w_kpe, z96, w_kpe[:, sw], z96,
         w[:, 544:1056] * DIFF_SCALE, w[:, 1056:1568], w[:, 1568:2080], w[:, 2080:3104], w[:, 3104:4128]],
        axis=1).astype(BF16)
    w_fold, w_f2 = _fold_weights(mla_w_uq[0], mla_w_uk[0], mla_w_uv[0], w_branch_mla[0])
    uq = mla_w_uq[0].reshape(MLA_QR, MLA_HEADS, MLA_NOPE + MLA_ROPE)
    w_pe = uq[:, :, MLA_NOPE:]
    w_q2 = jnp.concatenate([w_fold, w_pe.reshape(MLA_QR, -1), w_pe[:, :, sw].reshape(MLA_QR, -1)],
                           axis=1).astype(BF16)
    proj_w = (norm_attn_g[0].reshape(1, -1), w_in_r, mla_q_norm_g[0].reshape(1, -1), w_q2,
              mla_kv_norm_g[0].reshape(1, -1))
    post_w = (w_f2.astype(BF16), w_branch_diff[0].astype(BF16), w_o[0].astype(BF16),
              norm_ffn_g[0].reshape(1, -1), w_ffn_gate[0].astype(BF16), w_ffn_up[0].astype(BF16),
              w_ffn_down[0].astype(BF16), norm_final_g.reshape(1, -1))
    lam_vecs = tuple(v[0].reshape(1, -1) for v in (diff_lambda_q1, diff_lambda_k1, diff_lambda_q2, diff_lambda_k2))
    subg = diff_subln_g[0]

    tab_p = _rope_tables(jnp.arange(seq, dtype=jnp.int32))
    tab_s = _rope_tables(jnp.full((nb,), past_len, dtype=jnp.int32))
    bias_p, bias_dec, bias_self = _bias_tables(rel_bias_table, past_len)

    xp = x_prompt.reshape(batch * seq, D_MODEL)
    (p_ckv, p_kpe, p_dk, p_dv, p_kvcat, p_ckvt, p_qcat, p_dq, p_dk16, p_dvt, p_ga, p_gb) = _proj(
        xp, tab_p, 2 * BK, proj_w, kb=BK)
    xs = x_sample.reshape(nb, D_MODEL)
    (s_ckv, s_kpe, s_dk, s_dv, s_kvcat, _, s_qcat, s_dq, s_dk16, s_dvt, s_ga, s_gb) = _proj(
        xs, tab_s, nb, proj_w)

    caches = (cache_mla_ckv.reshape(n_pool, PAGE, MLA_KVR),
              jnp.transpose(cache_mla_kpe, (0, 1, 3, 2)).reshape(n_pool, MLA_ROPE, PAGE),
              jnp.transpose(cache_diff_k, (0, 1, 3, 4, 5, 2)).reshape(n_pool, 512, PAGE),
              cache_diff_v.reshape(n_pool, PAGE * DIFF_HEADS, DIFF_V))
    p_olat, p_odiff, s_olat, s_odiff = _attn_fused(
        (p_qcat, p_dq, p_kvcat, p_ckvt, p_dk16, p_dvt),
        (s_qcat, s_dq, s_kvcat, s_dk16, s_dv.reshape(nb, DIFF_HEADS * DIFF_V).astype(BF16)),
        caches, page_table, bias_p, bias_dec, bias_self, lam_vecs, subg, batch, seq)

    y_prompt = _post(xp, p_olat, p_odiff, p_ga, p_gb, post_w, 256).reshape(batch, seq, D_MODEL)
    y_sample = _post(xs, s_olat, s_odiff, s_ga, s_gb, post_w, nb).reshape(nb, 1, D_MODEL)

    def kpe_out(t, b, s):
        return jnp.transpose(t, (0, 2, 1)).reshape(1, b, s, MLA_ROPE)

    def dk_out(t, b, s):
        t = t.reshape(b, DIFF_HEADS, 2, DIFF_QK, s)
        return jnp.transpose(t, (0, 4, 1, 2, 3)).reshape(1, b, s, DIFF_HEADS, 2, DIFF_QK)

    return (y_prompt, y_sample,
            p_ckv.reshape(1, batch, seq, MLA_KVR), kpe_out(p_kpe, batch, seq),
            dk_out(p_dk, batch, seq), p_dv.reshape(1, batch, seq, DIFF_HEADS, DIFF_V),
            s_ckv.reshape(1, nb, 1, MLA_KVR), kpe_out(s_kpe, 1, nb).reshape(1, nb, 1, MLA_ROPE),
            dk_out(s_dk, 1, nb).reshape(1, nb, 1, DIFF_HEADS, 2, DIFF_QK),
            s_dv.reshape(1, nb, 1, DIFF_HEADS, DIFF_V))
```

```python
import functools
import math

import numpy as np
import jax
import jax.numpy as jnp
from jax import lax
from jax.experimental import pallas as pl
from jax.experimental.pallas import tpu as pltpu

F32 = jnp.float32
BF16 = jnp.bfloat16

D_MODEL = 1024
PAGE = 128
MLA_HEADS = 8
MLA_NOPE = 64
MLA_ROPE = 32
MLA_V = 64
MLA_QR = 384
MLA_KVR = 128
MLA_SCALE = (MLA_NOPE + MLA_ROPE) ** -0.5
ROPE_THETA = 10000.0
DIFF_HEADS = 4
DIFF_QK = 64
DIFF_V = 128
DIFF_SCALE = DIFF_QK ** -0.5
SUBLN_EPS = 1e-5
REL_BUCKETS = 32
REL_MAX_DIST = 128
FFN_DIM = 2816
NORM_EPS = 1e-6
LAM_INIT = 0.8 - 0.6 * math.exp(-0.3 * 0)

LANES = 128
QCAT = 2 * LANES
NEG = -1e30
LOG2E = math.log2(math.e)
VMEM_LIMIT = 56 * 1024 * 1024

C_QLAT = 0
C_KV = 384
C_KPE = 512
C_KPESW = 640
C_DQ = 768
C_DK = 1280
C_DV = 1792
C_GA = 2304
C_GB = 3328
C_END = 4352

BQ = 256
BK = 256
DEC_PAGES = 16
VT_ROWS = MLA_KVR + 16


def _cparams(sem):
    return pltpu.CompilerParams(dimension_semantics=sem, vmem_limit_bytes=VMEM_LIMIT)


def _const_spec(shape):
    nd = len(shape)
    return pl.BlockSpec(shape, lambda *_: (0,) * nd, pipeline_mode=pl.Buffered(1))


def _rope_tables_kernel(pos_ref, inv_ref, cq_ref, sq_ref, ck_ref, sk_ref):
    ang = pos_ref[...] * inv_ref[...]
    c = jnp.cos(ang)
    s = jnp.sin(ang)
    lane = lax.broadcasted_iota(jnp.int32, ang.shape, 1)
    first_half = (lane & (MLA_ROPE - 1)) < (MLA_ROPE // 2)
    s_signed = jnp.where(first_half, -s, s)
    cq_ref[...] = c
    sq_ref[...] = s_signed
    kmask = lax.broadcasted_iota(jnp.int32, ck_ref.shape, 1) < MLA_ROPE
    ck_ref[...] = jnp.where(kmask, c[:, :LANES], 0.0)
    sk_ref[...] = jnp.where(kmask, s_signed[:, :LANES], 0.0)


def _rope_tables(pos):
    n = pos.shape[0]
    half = MLA_ROPE // 2
    inv = (ROPE_THETA ** (-jnp.arange(half, dtype=F32) / half))
    inv = jnp.tile(inv, 2 * MLA_HEADS).reshape(1, 2 * LANES)
    return pl.pallas_call(
        _rope_tables_kernel,
        out_shape=(jax.ShapeDtypeStruct((n, 2 * LANES), F32), jax.ShapeDtypeStruct((n, 2 * LANES), F32),
                   jax.ShapeDtypeStruct((n, LANES), F32), jax.ShapeDtypeStruct((n, LANES), F32)),
        name="rope_tables",
    )(pos.astype(F32).reshape(n, 1), inv)


def _bucket(n):
    max_exact = REL_BUCKETS // 2
    nf = jnp.maximum(n, 1).astype(F32)
    large = max_exact + (jnp.log(nf / max_exact) / math.log(REL_MAX_DIST / max_exact)
                         * (REL_BUCKETS - max_exact)).astype(jnp.int32)
    large = jnp.minimum(large, REL_BUCKETS - 1)
    return jnp.where(n < max_exact, n, large)


def _bias_kernel(tab_ref, bp_ref, bd_ref, bs_ref, *, past_len):
    kidx = lax.broadcasted_iota(jnp.int32, (BK, BQ), 0)
    qidx = lax.broadcasted_iota(jnp.int32, (BK, BQ), 1)
    for d in range(3):
        rel = d * BQ + qidx - kidx
        b = _bucket(jnp.maximum(rel, 0))
        for h in range(DIFF_HEADS):
            val = jnp.zeros((BK, BQ), F32)
            for t in range(REL_BUCKETS):
                val = jnp.where(b == t, tab_ref[t, h] * LOG2E, val)
            if d == 0:
                val = jnp.where(rel >= 0, val, NEG)
            bp_ref[d * DIFF_HEADS + h] = val
    bp_ref[3 * DIFF_HEADS] = jnp.where(qidx - kidx >= 0, 0.0, NEG)
    gk = DEC_PAGES * PAGE
    row = lax.broadcasted_iota(jnp.int32, (2 * DIFF_HEADS, gk), 0)
    for g in range(past_len // gk):
        kpos = g * gk + lax.broadcasted_iota(jnp.int32, (2 * DIFF_HEADS, gk), 1)
        b = _bucket(jnp.maximum(past_len - kpos, 0))
        val = jnp.zeros((2 * DIFF_HEADS, gk), F32)
        for h in range(DIFF_HEADS):
            for t in range(REL_BUCKETS):
                val = jnp.where((b == t) & ((row >> 1) == h), tab_ref[t, h] * LOG2E, val)
        bd_ref[g] = val
    rows = lax.broadcasted_iota(jnp.int32, (2 * DIFF_HEADS, LANES), 0)
    b0 = _bucket(jnp.zeros((2 * DIFF_HEADS, LANES), jnp.int32))
    vs = jnp.zeros((2 * DIFF_HEADS, LANES), F32)
    for h in range(DIFF_HEADS):
        for t in range(REL_BUCKETS):
            vs = jnp.where((b0 == t) & ((rows >> 1) == h), tab_ref[t, h] * LOG2E, vs)
    bs_ref[...] = vs


def _bias_tables(table, past_len):
    return pl.pallas_call(
        functools.partial(_bias_kernel, past_len=past_len),
        in_specs=[pl.BlockSpec(memory_space=pltpu.SMEM)],
        out_shape=(jax.ShapeDtypeStruct((3 * DIFF_HEADS + 1, BK, BQ), F32),
                   jax.ShapeDtypeStruct((past_len // (DEC_PAGES * PAGE), 2 * DIFF_HEADS, DEC_PAGES * PAGE), F32),
                   jax.ShapeDtypeStruct((2 * DIFF_HEADS, LANES), F32)),
        compiler_params=pltpu.CompilerParams(vmem_limit_bytes=VMEM_LIMIT),
        name="bias_tables",
    )(table.astype(F32))


def _fold_kernel(nope_ref, uk_ref, uv_ref, bm_ref, wfold_ref, wf2_ref):
    for h in range(MLA_HEADS):
        wfold_ref[:, h * LANES:(h + 1) * LANES] = lax.dot_general(
            nope_ref[h], uk_ref[h], (((1,), (1,)), ((), ())),
            preferred_element_type=F32, precision=lax.Precision.HIGHEST)
        wf2_ref[h * LANES:(h + 1) * LANES, :] = jnp.dot(
            uv_ref[h], bm_ref[h], preferred_element_type=F32, precision=lax.Precision.HIGHEST)


def _fold_weights(w_uq, w_uk, w_uv, w_bm):
    uq = w_uq.reshape(MLA_QR, MLA_HEADS, MLA_NOPE + MLA_ROPE)
    nope = jnp.transpose(uq[:, :, :MLA_NOPE], (1, 0, 2))
    uk = jnp.transpose(w_uk, (1, 0, 2))
    uv = jnp.transpose(w_uv, (1, 0, 2))
    bm = w_bm.reshape(MLA_HEADS, MLA_V, D_MODEL)
    return pl.pallas_call(
        _fold_kernel,
        out_shape=(jax.ShapeDtypeStruct((MLA_QR, MLA_HEADS * LANES), F32),
                   jax.ShapeDtypeStruct((MLA_HEADS * LANES, D_MODEL), F32)),
        compiler_params=pltpu.CompilerParams(vmem_limit_bytes=VMEM_LIMIT),
        name="fold_weights",
    )(nope, uk, uv, bm)


def _rms(x, g, eps):
    return x * lax.rsqrt(jnp.mean(x * x, axis=-1, keepdims=True) + eps) * g


def _proj_kernel(x_ref, g_ref, win_ref, gq_ref, wq2_ref, gkv_ref, cq_ref, sq_ref, ck_ref, sk_ref,
                 ckv_ref, kpet_ref, dkt_ref, dv_ref, kvcat_ref, ckvt_ref, qcat_ref, dq_ref, dk16_ref, dvt_ref,
                 ga_ref, gb_ref):
    tm = x_ref.shape[0]
    h = _rms(x_ref[...], g_ref[...], NORM_EPS).astype(BF16)

    def seg(lo, hi):
        return jnp.dot(h, win_ref[:, lo:hi], preferred_element_type=F32)

    z0 = seg(C_QLAT, C_KPE)
    qn = _rms(z0[:, :C_KV], gq_ref[...], NORM_EPS).astype(BF16)
    zq = jnp.dot(qn, wq2_ref[...], preferred_element_type=F32)
    nq = MLA_HEADS * LANES
    qscale = MLA_SCALE * LOG2E
    qpe = (zq[:, nq:nq + 2 * LANES] * cq_ref[...] + zq[:, nq + 2 * LANES:] * sq_ref[...]) * qscale
    lane = lax.broadcasted_iota(jnp.int32, (tm, LANES), 1)
    per_block = LANES // MLA_ROPE
    for hd in range(MLA_HEADS):
        blk = qpe[:, (hd // per_block) * LANES:(hd // per_block + 1) * LANES]
        sh = (hd % per_block) * MLA_ROPE
        if sh:
            blk = pltpu.roll(blk, LANES - sh, axis=1)
        blk = jnp.where(lane < MLA_ROPE, blk, 0.0)
        qa16 = (zq[:, hd * LANES:(hd + 1) * LANES] * qscale).astype(BF16)
        qp16 = blk.astype(BF16)
        for c in range(qcat_ref.shape[0]):
            rows = slice(hd * (tm // qcat_ref.shape[0]), (hd + 1) * (tm // qcat_ref.shape[0]))
            tok = slice(c * (tm // qcat_ref.shape[0]), (c + 1) * (tm // qcat_ref.shape[0]))
            qcat_ref[c, rows, 0:LANES] = qa16[tok]
            qcat_ref[c, rows, LANES:QCAT] = qp16[tok]

    ckv = _rms(z0[:, C_KV:C_KPE], gkv_ref[...], NORM_EPS)
    zk = seg(C_KPE, C_DQ)
    kpe = zk[:, :LANES] * ck_ref[...] + zk[:, LANES:] * sk_ref[...]
    ckv_ref[...] = ckv
    kpet_ref[0] = kpe.T[:MLA_ROPE]
    kvcat_ref[:, :LANES] = ckv.astype(BF16)
    kvcat_ref[:, LANES:] = kpe.astype(BF16)
    nkb = ckvt_ref.shape[0]
    kb = tm // nkb
    ones_rows = jnp.ones((VT_ROWS - MLA_KVR, kb), BF16)
    for c in range(nkb):
        ckvt_ref[c, 0:MLA_KVR, :] = ckv[c * kb:(c + 1) * kb, :].T.astype(BF16)
        ckvt_ref[c, MLA_KVR:VT_ROWS, :] = ones_rows

    dq_ref[...] = (seg(C_DQ, C_DK) * LOG2E).astype(BF16)
    dk = seg(C_DK, C_DV)
    dkt_ref[0] = dk.T
    dk16_ref[...] = dk.astype(BF16)
    dv = seg(C_DV, C_GA)
    for hd in range(DIFF_HEADS):
        dv_ref[pl.ds(hd, tm, stride=DIFF_HEADS), :] = dv[:, hd * DIFF_V:(hd + 1) * DIFF_V]
    for c in range(nkb):
        for hd in range(DIFF_HEADS):
            dvt_ref[c, hd, 0:DIFF_V, :] = dv[c * kb:(c + 1) * kb, hd * DIFF_V:(hd + 1) * DIFF_V].T.astype(BF16)
            dvt_ref[c, hd, DIFF_V:VT_ROWS, :] = ones_rows

    ga_ref[...] = jax.nn.sigmoid(seg(C_GA, C_GB))
    gb_ref[...] = jax.nn.sigmoid(seg(C_GB, C_END))


def _proj(x, tables, tm, weights, kb=None):
    g_attn, w_in_r, g_q, w_q2, g_kv = weights
    cq, sq, ck, sk = tables
    n = x.shape[0]
    nb = n // tm
    kb = tm if kb is None else kb
    nkb = tm // kb
    seq = cq.shape[0]
    tb = seq // tm
    batch = n // seq
    tcol = lambda i: (i // tb, 0, i % tb)
    row = lambda i: (i, 0)
    trow = lambda i: (i % tb, 0)
    in_specs = [
        pl.BlockSpec((tm, D_MODEL), row),
        _const_spec((1, D_MODEL)),
        _const_spec((D_MODEL, C_END)),
        _const_spec((1, MLA_QR)),
        _const_spec((MLA_QR, w_q2.shape[1])),
        _const_spec((1, MLA_KVR)),
        pl.BlockSpec((tm, 2 * LANES), trow), pl.BlockSpec((tm, 2 * LANES), trow),
        pl.BlockSpec((tm, LANES), trow), pl.BlockSpec((tm, LANES), trow),
    ]
    out_shape = (
        jax.ShapeDtypeStruct((n, MLA_KVR), F32),
        jax.ShapeDtypeStruct((batch, MLA_ROPE, seq), F32),
        jax.ShapeDtypeStruct((batch, 2 * DIFF_HEADS * DIFF_QK, seq), F32),
        jax.ShapeDtypeStruct((n * DIFF_HEADS, DIFF_V), F32),
        jax.ShapeDtypeStruct((n, QCAT), BF16),
        jax.ShapeDtypeStruct((nb * nkb, VT_ROWS, kb), BF16),
        jax.ShapeDtypeStruct((nb * nkb, MLA_HEADS * kb, QCAT), BF16),
        jax.ShapeDtypeStruct((n, 2 * DIFF_HEADS * DIFF_QK), BF16),
        jax.ShapeDtypeStruct((n, 2 * DIFF_HEADS * DIFF_QK), BF16),
        jax.ShapeDtypeStruct((nb * nkb, DIFF_HEADS, VT_ROWS, kb), BF16),
        jax.ShapeDtypeStruct((n, D_MODEL), F32),
        jax.ShapeDtypeStruct((n, D_MODEL), F32),
    )
    out_specs = (
        pl.BlockSpec((tm, MLA_KVR), row),
        pl.BlockSpec((1, MLA_ROPE, tm), tcol),
        pl.BlockSpec((1, 512, tm), tcol),
        pl.BlockSpec((tm * DIFF_HEADS, DIFF_V), row),
        pl.BlockSpec((tm, QCAT), row),
        pl.BlockSpec((nkb, VT_ROWS, kb), lambda i: (i, 0, 0)),
        pl.BlockSpec((nkb, MLA_HEADS * kb, QCAT), lambda i: (i, 0, 0)),
        pl.BlockSpec((tm, 512), row),
        pl.BlockSpec((tm, 512), row),
        pl.BlockSpec((nkb, DIFF_HEADS, VT_ROWS, kb), lambda i: (i, 0, 0, 0)),
        pl.BlockSpec((tm, D_MODEL), row),
        pl.BlockSpec((tm, D_MODEL), row),
    )
    return pl.pallas_call(
        _proj_kernel,
        grid=(nb,),
        in_specs=in_specs,
        out_specs=out_specs,
        out_shape=out_shape,
        compiler_params=_cparams(("parallel",)),
        name="proj_in",
    )(x, g_attn, w_in_r, g_q, w_q2, g_kv, cq, sq, ck, sk)


def _lambda(lq1_ref, lk1_ref, lq2_ref, lk2_ref):
    a = jnp.sum(lq1_ref[...] * lk1_ref[...], axis=-1, keepdims=True)
    b = jnp.sum(lq2_ref[...] * lk2_ref[...], axis=-1, keepdims=True)
    return jnp.exp(a) - jnp.exp(b) + LAM_INIT


NQ_MLA = MLA_HEADS * BQ
NQ_DIFF = 2 * BQ
NQ_ALL = NQ_MLA + DIFF_HEADS * NQ_DIFF
GROUP_KEYS = DEC_PAGES * PAGE
N_SLOTS = 2


def _fused_kernel(pt_ref, sch_ref,
                  qcat_ref, dq_ref, kv_ref, ckvt_ref, dk_ref, dvt_ref, bias_ref,
                  lq1_ref, lk1_ref, lq2_ref, lk2_ref, subg_ref, subgr_ref,
                  sq_ref, sdq_ref, skv_ref, sdk_ref, sdv_ref, bd_ref, bs_ref,
                  cckv_hbm, ckpe_hbm, cdk_hbm, cdv_hbm,
                  olat_ref, odiff_ref, solat_ref, sodiff_ref,
                  qd_scr, m_scr, acc_scr,
                  ckv_buf, kpe_buf, dkt_buf, dv_buf, sem, ckv16_scr, kpe16_scr, dkt16_scr, v16_scr,
                  smm_scr, slm_scr, sam_scr, smd_scr, sld_scr, sad_scr,
                  *, n_pages, groups_per_batch):
    b = pl.program_id(0)
    i = pl.program_id(1)
    nq = pl.num_programs(1)
    step = b * nq + i
    n_steps = pl.num_programs(0) * nq
    groups_per_row = n_pages // DEC_PAGES
    n_total = pl.num_programs(0) * groups_per_batch
    g_base = b * groups_per_batch + sch_ref[i]
    n_dec = sch_ref[nq + i]
    nrow = 2 * DIFF_HEADS

    def group_copies(row, grp, slot):
        cps = []
        for k in range(DEC_PAGES):
            pg = pt_ref[row * n_pages + grp * DEC_PAGES + k]
            keys = pl.ds(k * PAGE, PAGE)
            cps.append(pltpu.make_async_copy(cckv_hbm.at[pg], ckv_buf.at[slot, keys, :], sem.at[slot, 0]))
            cps.append(pltpu.make_async_copy(ckpe_hbm.at[pg], kpe_buf.at[slot, :, keys], sem.at[slot, 1]))
            cps.append(pltpu.make_async_copy(cdk_hbm.at[pg], dkt_buf.at[slot, :, keys], sem.at[slot, 2]))
            cps.append(pltpu.make_async_copy(cdv_hbm.at[pg],
                                             dv_buf.at[slot, pl.ds(k * PAGE * DIFF_HEADS, PAGE * DIFF_HEADS), :],
                                             sem.at[slot, 3]))
        return cps

    def start_group(row, grp, slot):
        for n, c in enumerate(group_copies(row, grp, slot)):
            c.start(priority=n % 2)

    def wait_group(row, grp, slot):
        for c in group_copies(row, grp, slot):
            c.wait()

    @pl.when(step == 0)
    def _():
        start_group(0, 0, 0)
        start_group(1 // groups_per_row, 1 % groups_per_row, 1)

    def decode_group(g):
        gl = g_base + g
        row = gl // groups_per_row
        grp = gl % groups_per_row
        slot = gl % N_SLOTS
        wait_group(row, grp, slot)

        ckv16_scr[...] = ckv_buf[slot].astype(BF16)
        kpe16_scr[...] = kpe_buf[slot].astype(BF16)
        dkt16_scr[...] = dkt_buf[slot].astype(BF16)
        for hd in range(DIFF_HEADS):
            v16_scr[:, hd * DIFF_V:(hd + 1) * DIFF_V] = dv_buf[
                slot, pl.ds(hd, GROUP_KEYS, stride=DIFF_HEADS), :].astype(BF16)
        gn = jnp.minimum(gl + N_SLOTS, n_total - 1)
        start_group(gn // groups_per_row, gn % groups_per_row, slot)

        q = sq_ref[pl.ds(row * MLA_HEADS, MLA_HEADS), :].astype(BF16)
        qa = q[:, :MLA_KVR]
        qp = q[:, MLA_KVR:MLA_KVR + MLA_ROPE]
        rowi = lax.broadcasted_iota(jnp.int32, (nrow, 512), 0)
        lanei = lax.broadcasted_iota(jnp.int32, (nrow, 512), 1)
        dq_b = jnp.broadcast_to(sdq_ref[pl.ds(row, 1), :], (nrow, 512))
        qbd = jnp.where((lanei >> 6) == rowi, dq_b, 0.0).astype(BF16)

        first = grp == 0
        kvs = skv_ref[pl.ds(row, 1), :]
        smm_scr[...] = jnp.where(first, jnp.sum(q.astype(F32) * kvs, axis=1, keepdims=True), smm_scr[...])
        slm_scr[...] = jnp.where(first, 1.0, slm_scr[...])
        sam_scr[...] = jnp.where(first, jnp.broadcast_to(kvs[:, :MLA_KVR], sam_scr.shape), sam_scr[...])
        s0 = jnp.sum(qbd.astype(F32) * sdk_ref[pl.ds(row, 1), :], axis=1, keepdims=True)
        smd_scr[...] = jnp.where(first, s0 + bs_ref[:, 0:1], smd_scr[...])
        sld_scr[...] = jnp.where(first, 1.0, sld_scr[...])
        sad_scr[...] = jnp.where(first, jnp.broadcast_to(sdv_ref[pl.ds(row, 1), :], sad_scr.shape), sad_scr[...])

        tdims = (((1,), (1,)), ((), ()))
        ckv16 = ckv16_scr[...]
        sm = (lax.dot_general(qa, ckv16, tdims, preferred_element_type=F32)
              + jnp.dot(qp, kpe16_scr[...], preferred_element_type=F32))
        sd = jnp.dot(qbd, dkt16_scr[...], preferred_element_type=F32) + bd_ref[grp]

        def merge(s, m_scr, l_scr):
            m_old = m_scr[...]
            m_new = jnp.maximum(m_old, jnp.max(s, axis=1, keepdims=True))
            a = jnp.exp2(m_old - m_new)
            p = jnp.exp2(s - m_new)
            l_scr[...] = l_scr[...] * a + jnp.sum(p, axis=1, keepdims=True)
            m_scr[...] = m_new
            return a, p.astype(BF16)

        a, p = merge(sm, smm_scr, slm_scr)
        sam_scr[...] = sam_scr[...] * a + jnp.dot(p, ckv16, preferred_element_type=F32)
        a, p = merge(sd, smd_scr, sld_scr)
        sad_scr[...] = sad_scr[...] * a + jnp.dot(p, v16_scr[...], preferred_element_type=F32)

        solat_ref[pl.ds(row * MLA_HEADS, MLA_HEADS), :] = sam_scr[...] / slm_scr[...]
        lam = _lambda(lq1_ref, lk1_ref, lq2_ref, lk2_ref)
        o = sad_scr[...] / sld_scr[...]
        for hd in range(DIFF_HEADS):
            o1 = o[2 * hd:2 * hd + 1, hd * DIFF_V:(hd + 1) * DIFF_V]
            o2 = o[2 * hd + 1:2 * hd + 2, hd * DIFF_V:(hd + 1) * DIFF_V]
            y = _rms(o1 - lam * o2, subgr_ref[...], SUBLN_EPS) * (1.0 - LAM_INIT)
            sodiff_ref[pl.ds(row * DIFF_HEADS + hd, 1), :] = y

    lane = lax.broadcasted_iota(jnp.int32, (BQ, LANES), 1)
    for hd in range(DIFF_HEADS):
        dqh = dq_ref[:, hd * LANES:(hd + 1) * LANES]
        qd_scr[hd, 0:BQ, :] = jnp.where(lane < DIFF_QK, dqh, jnp.zeros_like(dqh))
        qd_scr[hd, BQ:2 * BQ, :] = jnp.where(lane >= DIFF_QK, dqh, jnp.zeros_like(dqh))

    m_scr[...] = jnp.full(m_scr.shape, NEG, F32)
    acc_scr[...] = jnp.zeros(acc_scr.shape, F32)
    tdims = (((1,), (1,)), ((), ()))

    def block(j, diag):
        koff = pl.multiple_of(j * BK, BK)
        st = lax.dot_general(kv_ref[pl.ds(koff, BK), :], qcat_ref[0], tdims, preferred_element_type=F32)
        if diag:
            mask = bias_ref[3 * DIFF_HEADS]
            st = st + jnp.concatenate([mask] * MLA_HEADS, axis=1)
        d = 0 if diag else jnp.minimum(i - j, 2)
        parts = [st]
        for hd in range(DIFF_HEADS):
            kb = dk_ref[pl.ds(koff, BK), hd * LANES:(hd + 1) * LANES]
            b1 = bias_ref[d * DIFF_HEADS + hd]
            sh = lax.dot_general(kb, qd_scr[hd], tdims, preferred_element_type=F32)
            parts.append(sh + jnp.concatenate([b1, b1], axis=1))
        st = jnp.concatenate(parts, axis=1)
        m_old = m_scr[...]
        m_new = jnp.maximum(m_old, jnp.max(st, axis=0, keepdims=True))
        a = jnp.exp2(m_old - m_new)
        pt = jnp.exp2(st - m_new).astype(BF16)
        m_scr[...] = m_new
        cs = slice(0, NQ_MLA)
        acc_scr[:, cs] = acc_scr[:, cs] * a[:, cs] + jnp.dot(ckvt_ref[j], pt[:, cs], preferred_element_type=F32)
        for hd in range(DIFF_HEADS):
            cs = slice(NQ_MLA + hd * NQ_DIFF, NQ_MLA + (hd + 1) * NQ_DIFF)
            acc_scr[:, cs] = acc_scr[:, cs] * a[:, cs] + jnp.dot(dvt_ref[j, hd], pt[:, cs],
                                                               preferred_element_type=F32)

    def paired(j, carry):
        decode_group(j)
        block(j, False)
        return carry

    def alone(j, carry):
        block(j, False)
        return carry

    def tail(j, carry):
        decode_group(j)
        return carry

    n_paired = jnp.minimum(i, n_dec)
    lax.fori_loop(0, n_paired, paired, 0)
    lax.fori_loop(n_paired, i, alone, 0)

    @pl.when(i < n_dec)
    def _():
        decode_group(i)

    block(i, True)
    lax.fori_loop(i + 1, n_dec, tail, 0)

    @pl.when(step == n_steps - 1)
    def _():
        last = n_total - 1
        for s_ in range(N_SLOTS):
            wait_group(last // groups_per_row, last % groups_per_row, s_)

    acc = acc_scr[...]
    o = acc[:MLA_KVR] / acc[MLA_KVR:MLA_KVR + 1]
    for hd in range(MLA_HEADS):
        olat_ref[:, hd * LANES:(hd + 1) * LANES] = o[:, hd * BQ:(hd + 1) * BQ].T.astype(BF16)
    lam = _lambda(lq1_ref, lk1_ref, lq2_ref, lk2_ref)
    for hd in range(DIFF_HEADS):
        c0 = NQ_MLA + hd * NQ_DIFF
        od = o[:, c0:c0 + BQ] - lam * o[:, c0 + BQ:c0 + 2 * BQ]
        y = od * lax.rsqrt(jnp.mean(od * od, axis=0, keepdims=True) + SUBLN_EPS) * subg_ref[...]
        odiff_ref[:, hd * LANES:(hd + 1) * LANES] = (y * (1.0 - LAM_INIT)).T.astype(BF16)


def _attn_fused(p_in, s_in, caches, page_table, bias, bias_dec, bias_self, lam_vecs, subg, batch, seq):
    qcat, dq, kvcat, ckvt, dk16, dvt = p_in
    s_qcat, s_dq, s_kvcat, s_dk16, s_dv16 = s_in
    nq = seq // BQ
    nk = seq // BK
    n = batch * seq
    nb, n_pages = page_table.shape
    assert n_pages % DEC_PAGES == 0 and (nb * (n_pages // DEC_PAGES)) % batch == 0
    gpb = nb * (n_pages // DEC_PAGES) // batch
    blocks = np.cumsum(np.arange(1, nq + 1))
    cum = np.round(gpb * blocks / blocks[-1]).astype(np.int64)
    cnt = np.diff(np.concatenate([[0], cum]))
    sched = jnp.asarray(np.concatenate([cum - cnt, cnt]), dtype=jnp.int32)
    pt = page_table.reshape(-1).astype(jnp.int32)

    qrow = lambda b, i, pt, sch: (b * nq + i, 0)

    def const(shape):
        nd = len(shape)
        return pl.BlockSpec(shape, lambda b, i, pt, sch: (0,) * nd, pipeline_mode=pl.Buffered(1))

    in_specs = [
        pl.BlockSpec((1, NQ_MLA, QCAT), lambda b, i, pt, sch: (b * nq + i, 0, 0)),
        pl.BlockSpec((BQ, 512), qrow),
        pl.BlockSpec((seq, QCAT), lambda b, i, pt, sch: (b, 0), pipeline_mode=pl.Buffered(1)),
        pl.BlockSpec((nk, VT_ROWS, BK), lambda b, i, pt, sch: (b, 0, 0), pipeline_mode=pl.Buffered(1)),
        pl.BlockSpec((seq, 512), lambda b, i, pt, sch: (b, 0), pipeline_mode=pl.Buffered(1)),
        pl.BlockSpec((nk, DIFF_HEADS, VT_ROWS, BK), lambda b, i, pt, sch: (b, 0, 0, 0),
                     pipeline_mode=pl.Buffered(1)),
        const(bias.shape),
        const((1, DIFF_QK)), const((1, DIFF_QK)), const((1, DIFF_QK)), const((1, DIFF_QK)),
        const((DIFF_V, 1)), const((1, DIFF_V)),
        const((nb * MLA_HEADS, QCAT)), const((nb, 512)), const((nb, QCAT)), const((nb, 512)), const((nb, 512)),
        const(bias_dec.shape),
        const(bias_self.shape),
        pl.BlockSpec(memory_space=pl.ANY), pl.BlockSpec(memory_space=pl.ANY),
        pl.BlockSpec(memory_space=pl.ANY), pl.BlockSpec(memory_space=pl.ANY),
    ]
    out_specs = (pl.BlockSpec((BQ, MLA_HEADS * LANES), qrow), pl.BlockSpec((BQ, DIFF_HEADS * DIFF_V), qrow),
                 const((nb * MLA_HEADS, LANES)), const((nb * DIFF_HEADS, DIFF_V)))
    out_shape = (jax.ShapeDtypeStruct((n, MLA_HEADS * LANES), BF16),
                 jax.ShapeDtypeStruct((n, DIFF_HEADS * DIFF_V), BF16),
                 jax.ShapeDtypeStruct((nb * MLA_HEADS, LANES), F32),
                 jax.ShapeDtypeStruct((nb * DIFF_HEADS, DIFF_V), F32))
    nrow = 2 * DIFF_HEADS
    scratch = [
        pltpu.VMEM((DIFF_HEADS, NQ_DIFF, LANES), BF16),
        pltpu.VMEM((1, NQ_ALL), F32), pltpu.VMEM((VT_ROWS, NQ_ALL), F32),
        pltpu.VMEM((N_SLOTS, GROUP_KEYS, MLA_KVR), F32),
        pltpu.VMEM((N_SLOTS, MLA_ROPE, GROUP_KEYS), F32),
        pltpu.VMEM((N_SLOTS, 512, GROUP_KEYS), F32),
        pltpu.VMEM((N_SLOTS, GROUP_KEYS * DIFF_HEADS, DIFF_V), F32),
        pltpu.SemaphoreType.DMA((N_SLOTS, 4)),
        pltpu.VMEM((GROUP_KEYS, MLA_KVR), BF16), pltpu.VMEM((MLA_ROPE, GROUP_KEYS), BF16),
        pltpu.VMEM((512, GROUP_KEYS), BF16), pltpu.VMEM((GROUP_KEYS, DIFF_HEADS * DIFF_V), BF16),
        pltpu.VMEM((MLA_HEADS, 1), F32), pltpu.VMEM((MLA_HEADS, 1), F32), pltpu.VMEM((MLA_HEADS, MLA_KVR), F32),
        pltpu.VMEM((nrow, 1), F32), pltpu.VMEM((nrow, 1), F32), pltpu.VMEM((nrow, 512), F32),
    ]
    grid_spec = pltpu.PrefetchScalarGridSpec(
        num_scalar_prefetch=2, grid=(batch, nq), in_specs=in_specs, out_specs=out_specs, scratch_shapes=scratch)
    olat, odiff, s_olat, s_odiff = pl.pallas_call(
        functools.partial(_fused_kernel, n_pages=n_pages, groups_per_batch=gpb),
        grid_spec=grid_spec,
        out_shape=out_shape,
        compiler_params=_cparams(("arbitrary", "arbitrary")),
        name="attn_fused",
    )(pt, sched, qcat, dq, kvcat, ckvt, dk16, dvt, bias, *lam_vecs, subg.reshape(-1, 1), subg.reshape(1, -1),
      jnp.transpose(s_qcat.reshape(MLA_HEADS, nb, QCAT), (1, 0, 2)).reshape(nb * MLA_HEADS, QCAT).astype(F32), s_dq.astype(F32), s_kvcat.astype(F32),
      s_dk16.astype(F32), s_dv16.astype(F32), bias_dec, bias_self, *caches)
    return (olat, odiff, s_olat.reshape(nb, MLA_HEADS * LANES).astype(BF16),
            s_odiff.reshape(nb, DIFF_HEADS * DIFF_V).astype(BF16))


def _post_kernel(x_ref, olat_ref, odiff_ref, ga_ref, gb_ref, wf2_ref, wbd_ref, wo_ref,
                 gffn_ref, wg_ref, wu_ref, wd_ref, gfin_ref, y_ref):
    bm = jnp.dot(olat_ref[...], wf2_ref[...], preferred_element_type=F32)
    bd = jnp.dot(odiff_ref[...], wbd_ref[...], preferred_element_type=F32)
    merged = (ga_ref[...] * bm + gb_ref[...] * bd).astype(BF16)
    x = x_ref[...] + jnp.dot(merged, wo_ref[...], preferred_element_type=F32)
    h2 = _rms(x, gffn_ref[...], NORM_EPS).astype(BF16)
    gate = jnp.dot(h2, wg_ref[...], preferred_element_type=F32)
    up = jnp.dot(h2, wu_ref[...], preferred_element_type=F32)
    act = (gate * jax.nn.sigmoid(gate) * up).astype(BF16)
    x = x + jnp.dot(act, wd_ref[...], preferred_element_type=F32)
    y_ref[...] = _rms(x, gfin_ref[...], NORM_EPS)


def _post(x, olat, odiff, ga, gb, weights, tm):
    n = x.shape[0]
    row = lambda i: (i, 0)
    wf2, wbd, wo, gffn, wg, wu, wd, gfin = weights
    single = pl.Buffered(1)

    def wspec(shape):
        return pl.BlockSpec(shape, lambda i: (0, 0), pipeline_mode=single)

    in_specs = [
        pl.BlockSpec((tm, D_MODEL), row),
        pl.BlockSpec((tm, MLA_HEADS * LANES), row),
        pl.BlockSpec((tm, 512), row),
        pl.BlockSpec((tm, D_MODEL), row),
        pl.BlockSpec((tm, D_MODEL), row),
        wspec(wf2.shape), wspec(wbd.shape), wspec(wo.shape), wspec(gffn.shape),
        wspec(wg.shape), wspec(wu.shape), wspec(wd.shape), wspec(gfin.shape),
    ]
    return pl.pallas_call(
        _post_kernel,
        grid=(n // tm,),
        in_specs=in_specs,
        out_specs=pl.BlockSpec((tm, D_MODEL), row),
        out_shape=jax.ShapeDtypeStruct((n, D_MODEL), F32),
        compiler_params=_cparams(("parallel",)),
        name="merge_ffn",
    )(x, olat, odiff, ga, gb, wf2, wbd, wo, gffn, wg, wu, wd, gfin)


def kernel(x_prompt, x_sample, cache_mla_ckv, cache_mla_kpe, cache_diff_k, cache_diff_v, page_table, rel_bias_table, norm_attn_g, w_in, mla_q_norm_g, mla_w_uq, mla_kv_norm_g, mla_w_uk, mla_w_uv, diff_lambda_q1, diff_lambda_k1, diff_lambda_q2, diff_lambda_k2, diff_subln_g, w_branch_mla, w_branch_diff, w_o, norm_ffn_g, w_ffn_gate, w_ffn_up, w_ffn_down, norm_final_g):
    batch, seq, _ = x_prompt.shape
    nb, dec_seq, _ = x_sample.shape
    assert dec_seq == 1 and w_in.shape[0] == 1
    n_pages = page_table.shape[1]
    past_len = n_pages * PAGE
    assert REL_MAX_DIST <= BQ + 1 and seq % BQ == 0
    n_pool = cache_mla_ckv.shape[1]

    w = w_in[0]
    sw = np.concatenate([np.arange(16, 32), np.arange(0, 16)])
    z96 = jnp.zeros((D_MODEL, LANES - MLA_ROPE), F32)
    w_kpe = w[:, 512:544]
    w_in_r = jnp.concatenate(
        [w[:, 0:384], w[:, 384:512], w_kpe, z96, w_kpe[:, sw], z96,
         w[:, 544:1056] * DIFF_SCALE, w[:, 1056:1568], w[:, 1568:2080], w[:, 2080:3104], w[:, 3104:4128]],
        axis=1).astype(BF16)
    w_fold, w_f2 = _fold_weights(mla_w_uq[0], mla_w_uk[0], mla_w_uv[0], w_branch_mla[0])
    uq = mla_w_uq[0].reshape(MLA_QR, MLA_HEADS, MLA_NOPE + MLA_ROPE)
    w_pe = uq[:, :, MLA_NOPE:]
    w_q2 = jnp.concatenate([w_fold, w_pe.reshape(MLA_QR, -1), w_pe[:, :, sw].reshape(MLA_QR, -1)],
                           axis=1).astype(BF16)
    proj_w = (norm_attn_g[0].reshape(1, -1), w_in_r, mla_q_norm_g[0].reshape(1, -1), w_q2,
              mla_kv_norm_g[0].reshape(1, -1))
    post_w = (w_f2.astype(BF16), w_branch_diff[0].astype(BF16), w_o[0].astype(BF16),
              norm_ffn_g[0].reshape(1, -1), w_ffn_gate[0].astype(BF16), w_ffn_up[0].astype(BF16),
              w_ffn_down[0].astype(BF16), norm_final_g.reshape(1, -1))
    lam_vecs = tuple(v[0].reshape(1, -1) for v in (diff_lambda_q1, diff_lambda_k1, diff_lambda_q2, diff_lambda_k2))
    subg = diff_subln_g[0]

    tab_p = _rope_tables(jnp.arange(seq, dtype=jnp.int32))
    tab_s = _rope_tables(jnp.full((nb,), past_len, dtype=jnp.int32))
    bias_p, bias_dec, bias_self = _bias_tables(rel_bias_table, past_len)

    xp = x_prompt.reshape(batch * seq, D_MODEL)
    (p_ckv, p_kpe, p_dk, p_dv, p_kvcat, p_ckvt, p_qcat, p_dq, p_dk16, p_dvt, p_ga, p_gb) = _proj(
        xp, tab_p, 2 * BK, proj_w, kb=BK)
    xs = x_sample.reshape(nb, D_MODEL)
    (s_ckv, s_kpe, s_dk, s_dv, s_kvcat, _, s_qcat, s_dq, s_dk16, s_dvt, s_ga, s_gb) = _proj(
        xs, tab_s, nb, proj_w)

    caches = (cache_mla_ckv.reshape(n_pool, PAGE, MLA_KVR),
              jnp.transpose(cache_mla_kpe, (0, 1, 3, 2)).reshape(n_pool, MLA_ROPE, PAGE),
              jnp.transpose(cache_diff_k, (0, 1, 3, 4, 5, 2)).reshape(n_pool, 512, PAGE),
              cache_diff_v.reshape(n_pool, PAGE * DIFF_HEADS, DIFF_V))
    p_olat, p_odiff, s_olat, s_odiff = _attn_fused(
        (p_qcat, p_dq, p_kvcat, p_ckvt, p_dk16, p_dvt),
        (s_qcat, s_dq, s_kvcat, s_dk16, s_dv.reshape(nb, DIFF_HEADS * DIFF_V).astype(BF16)),
        caches, page_table, bias_p, bias_dec, bias_self, lam_vecs, subg, batch, seq)

    y_prompt = _post(xp, p_olat, p_odiff, p_ga, p_gb, post_w, 256).reshape(batch, seq, D_MODEL)
    y_sample = _post(xs, s_olat, s_odiff, s_ga, s_gb, post_w, nb).reshape(nb, 1, D_MODEL)

    def kpe_out(t, b, s):
        return jnp.transpose(t, (0, 2, 1)).reshape(1, b, s, MLA_ROPE)

    def dk_out(t, b, s):
        t = t.reshape(b, DIFF_HEADS, 2, DIFF_QK, s)
        return jnp.transpose(t, (0, 4, 1, 2, 3)).reshape(1, b, s, DIFF_HEADS, 2, DIFF_QK)

    return (y_prompt, y_sample,
            p_ckv.reshape(1, batch, seq, MLA_KVR), kpe_out(p_kpe, batch, seq),
            dk_out(p_dk, batch, seq), p_dv.reshape(1, batch, seq, DIFF_HEADS, DIFF_V),
            s_ckv.reshape(1, nb, 1, MLA_KVR), kpe_out(s_kpe, 1, nb).reshape(1, nb, 1, MLA_ROPE),
            dk_out(s_dk, 1, nb).reshape(1, nb, 1, DIFF_HEADS, 2, DIFF_QK),
            s_dv.reshape(1, nb, 1, DIFF_HEADS, DIFF_V))
```
